```python
import jax
import jax.numpy as jnp
from jax import lax
import numpy as np

D_MODEL = 1024
BATCH = 8
SEQ = 4096
DEPTH = 4

GRID_W = 64
CTX_LEN = 256
DN_HEADS = 8
DN_DK = 64
DN_W = DN_HEADS * DN_DK
DN_CHUNK = 64
CONV_K = 5
A_MIN = 1.0
A_MAX = 16.0
DT_MIN = 1e-3
DT_MAX = 1e-1
NA_HEADS = 8
NA_DH = 64
NA_W = NA_HEADS * NA_DH
WIN_R = 8
WIN_C = 16
QB_W = 16
CB_W = QB_W + WIN_C
ROPE_BASE = 10000.0
PEER_HEADS = 8
N_KEYS = 128
N_EXPERTS = N_KEYS * N_KEYS
PEER_DK = 128
PEER_TOPK = 16
PEER_BLOCK = 128
IN_W = 3 * DN_W + DN_W + 4 * DN_HEADS + 3 * NA_W + 2 * D_MODEL
N_MOD = 6
EPS = 1e-6
NEG_INF = -1e30

kernel_name = 'hybrid_deltanet_natten_peer_dit'


def _rmsnorm(x, w):
    xf = x.astype(jnp.float32)
    y = xf * lax.rsqrt(jnp.mean(xf * xf, axis=-1, keepdims=True) + EPS)
    return (y * w.astype(jnp.float32)).astype(x.dtype)


def _l2norm(x):
    xf = x.astype(jnp.float32)
    return xf * lax.rsqrt(jnp.sum(xf * xf, axis=-1, keepdims=True) + EPS)


def _modulate(h, shift, scale):
    return h * (1 + scale) + shift


def _axial_rotary(x):
    L, hd = x.shape[1], x.shape[-1]
    n_freq = hd // 4
    t = jnp.arange(L)
    row = (t // GRID_W).astype(jnp.float32)
    col = (t % GRID_W).astype(jnp.float32)
    inv_freq = ROPE_BASE ** (-jnp.arange(n_freq, dtype=jnp.float32) / n_freq)
    ang = jnp.concatenate([row[:, None] * inv_freq, col[:, None] * inv_freq], axis=-1)
    cos = jnp.cos(ang)[None, :, None, :].astype(x.dtype)
    sin = jnp.sin(ang)[None, :, None, :].astype(x.dtype)
    x1, x2 = jnp.split(x, 2, axis=-1)
    return jnp.concatenate([x1 * cos - x2 * sin, x2 * cos + x1 * sin], axis=-1)


def _short_conv(x, w):
    ch = x.shape[-1]
    y = lax.conv_general_dilated(
        x, w[:, None, :].astype(x.dtype), window_strides=(1,),
        padding=[(CONV_K // 2, CONV_K // 2)],
        dimension_numbers=('NWC', 'WIO', 'NWC'), feature_group_count=ch)
    return jax.nn.silu(y)


def _flip_if(t, rev):
    return jnp.flip(t, axis=2) if rev else t


def _gated_delta_chunked(q, k, v, g, beta, s0):
    B, H, L, dk = q.shape
    dv = v.shape[-1]
    n = L // DN_CHUNK

    def to_chunks(t):
        return t.reshape(B, H, n, DN_CHUNK, *t.shape[3:])

    q = to_chunks(q * dk ** -0.5)
    k = to_chunks(k)
    v = to_chunks(v)
    beta = to_chunks(beta)
    g = jnp.cumsum(to_chunks(g), axis=-1)
    incl = jnp.tril(jnp.ones((DN_CHUNK, DN_CHUNK), dtype=bool))
    strict = jnp.tril(jnp.ones((DN_CHUNK, DN_CHUNK), dtype=bool), k=-1)
    diff = g[..., :, None] - g[..., None, :]
    decay = jnp.where(incl, jnp.exp(jnp.where(incl, diff, 0.0)), 0.0)
    k_beta = k * beta[..., None]
    a_mat = jnp.where(strict, jnp.einsum('bhncd,bhnsd->bhncs', k_beta, k) * decay, 0.0)
    eye = jnp.eye(DN_CHUNK, dtype=jnp.float32)
    t_mat = lax.linalg.triangular_solve(
        eye + a_mat, jnp.broadcast_to(eye, a_mat.shape),
        left_side=True, lower=True, unit_diagonal=True)
    w_val = jnp.einsum('bhncs,bhnse->bhnce', t_mat, v * beta[..., None])
    u_key = jnp.einsum('bhncs,bhnsd->bhncd', t_mat, k_beta * jnp.exp(g)[..., None])
    intra = jnp.where(incl, jnp.einsum('bhncd,bhnsd->bhncs', q, k) * decay, 0.0)
    g_last = g[..., -1]
    q_dec = q * jnp.exp(g)[..., None]
    k_dec = k * jnp.exp(g_last[..., None] - g)[..., None]
    xs = tuple(jnp.moveaxis(t, 2, 0) for t in (q_dec, k_dec, w_val, u_key, intra, g_last))

    def step(s, inp):
        qd, kd, wv, uk, att, gl = inp
        v_new = wv - jnp.einsum('bhcd,bhde->bhce', uk, s)
        o = jnp.einsum('bhcd,bhde->bhce', qd, s) + jnp.einsum('bhcs,bhse->bhce', att, v_new)
        s = s * jnp.exp(gl)[..., None, None] + jnp.einsum('bhcd,bhce->bhde', kd, v_new)
        return s, o

    s_final, o = lax.scan(step, s0, xs)
    o = jnp.moveaxis(o, 0, 2).reshape(B, H, L, dv)
    return o, s_final


def _bidir_delta(qc, kc, vc, gc, bc, qx, kx, vx, gx, bx):
    B, H, _, dk = qc.shape
    s0 = jnp.zeros((B, H, dk, vc.shape[-1]), jnp.float32)
    out_c, out_x = [], []
    for d in range(2):
        rev = d == 1
        oc, sc = _gated_delta_chunked(_flip_if(qc, rev), _flip_if(kc, rev), _flip_if(vc, rev),
                                      _flip_if(gc[d], rev), _flip_if(bc[d], rev), s0)
        ox, _ = _gated_delta_chunked(_flip_if(qx, rev), _flip_if(kx, rev), _flip_if(vx, rev),
                                     _flip_if(gx[d], rev), _flip_if(bx[d], rev), sc)
        out_c.append(_flip_if(oc, rev))
        out_x.append(_flip_if(ox, rev))
    return out_c[0] + out_c[1], out_x[0] + out_x[1]


def _delta_inputs(qkv, a, b, conv_w, a_log, dt_bias, rotary):
    B, L, _ = qkv.shape
    qkv = _short_conv(qkv, conv_w)
    q, k, v = (t.reshape(B, L, DN_HEADS, DN_DK) for t in jnp.split(qkv, 3, axis=-1))
    q, k = _l2norm(q), _l2norm(k)
    if rotary:
        q, k = _axial_rotary(q), _axial_rotary(k)

    def per_dir(t):
        return t.astype(jnp.float32).reshape(B, L, 2, DN_HEADS).transpose(2, 0, 3, 1)

    g = -jnp.exp(a_log.astype(jnp.float32))[:, None, :, None] * jax.nn.softplus(
        per_dir(a) + dt_bias.astype(jnp.float32)[:, None, :, None])
    beta = jax.nn.sigmoid(per_dir(b))

    def to_bhl(t):
        return t.astype(jnp.float32).transpose(0, 2, 1, 3)

    return to_bhl(q), to_bhl(k), to_bhl(v), g, beta


def _gated_head_norm(o, z, w):
    B, H, L, dv = o.shape
    o = _rmsnorm(o.transpose(0, 2, 1, 3), w)
    gate = jax.nn.silu(z.reshape(B, L, H, dv).astype(jnp.float32))
    return (o * gate).reshape(B, L, H * dv).astype(z.dtype)


def _na_inputs(qkv, q_w, k_w):
    B, L, _ = qkv.shape
    q, k, v = (t.reshape(B, L, NA_HEADS, NA_DH) for t in jnp.split(qkv, 3, axis=-1))
    q, k = _rmsnorm(q, q_w), _rmsnorm(k, k_w)
    return tuple(t.transpose(0, 2, 1, 3) for t in (q, k, v))


def _column_bands():
    ncb = GRID_W // QB_W
    q_col = np.arange(ncb)[:, None] * QB_W + np.arange(QB_W)[None, :]
    band_start = np.clip(np.arange(ncb) * QB_W - WIN_C // 2, 0, GRID_W - CB_W)
    key_col = band_start[:, None] + np.arange(CB_W)[None, :]
    win_start = np.clip(q_col - WIN_C // 2, 0, GRID_W - WIN_C)
    rel = key_col[:, None, :] - win_start[:, :, None]
    mask = (rel >= 0) & (rel < WIN_C)
    off = np.clip(key_col[:, None, :] - q_col[:, :, None] + (WIN_C - 1), 0, 2 * WIN_C - 2)
    return key_col, mask, off


def _neighbourhood_attention(q, k, v, k_ctx, v_ctx, rpb):
    B, H, L, hd = q.shape
    rows = L // GRID_W
    wr = min(WIN_R, rows)
    ncb = GRID_W // QB_W
    n_loc = wr * CB_W
    key_col, col_mask, col_off = _column_bands()
    mask = np.broadcast_to(col_mask[:, :, None, :], (ncb, QB_W, wr, CB_W)).reshape(ncb, QB_W, n_loc)
    qg = q.reshape(B, H, rows, GRID_W, hd)
    kg = k.reshape(B, H, rows, GRID_W, hd)
    vg = v.reshape(B, H, rows, GRID_W, hd)
    rpb_c = rpb[:, :, col_off]
    scale = hd ** -0.5

    def gather_band(t, rs):
        band = lax.dynamic_slice_in_dim(t, rs, wr, axis=2)[:, :, :, key_col]
        return band.transpose(0, 1, 3, 2, 4, 5).reshape(B, H, ncb, n_loc, hd)

    def row_block(r):
        rs = jnp.clip(r - wr // 2, 0, rows - wr)
        qb = lax.dynamic_index_in_dim(qg, r, axis=2, keepdims=False).reshape(B, H, ncb, QB_W, hd)
        kb, vb = gather_band(kg, rs), gather_band(vg, rs)
        dr = rs - r + jnp.arange(wr) + (WIN_R - 1)
        bias = rpb_c[:, dr].transpose(0, 2, 3, 1, 4).reshape(H, ncb, QB_W, n_loc)
        s_loc = jnp.einsum('bhjqd,bhjkd->bhjqk', qb, kb).astype(jnp.float32) * scale + bias
        s_loc = jnp.where(mask, s_loc, NEG_INF)
        s_ctx = jnp.einsum('bhjqd,bhkd->bhjqk', qb, k_ctx).astype(jnp.float32) * scale
        p = jax.nn.softmax(jnp.concatenate([s_loc, s_ctx], axis=-1), axis=-1).astype(v.dtype)
        o = (jnp.einsum('bhjqk,bhjkd->bhjqd', p[..., :n_loc], vb)
             + jnp.einsum('bhjqk,bhkd->bhjqd', p[..., n_loc:], v_ctx))
        return o.reshape(B, H, GRID_W, hd)

    o = lax.map(row_block, jnp.arange(rows))
    return o.transpose(1, 0, 3, 2, 4).reshape(B, L, H * hd)


def _context_attention(q, k, v):
    B, H, n, hd = q.shape
    s = jnp.einsum('bhqd,bhkd->bhqk', q, k).astype(jnp.float32) * hd ** -0.5
    p = jax.nn.softmax(s, axis=-1).astype(v.dtype)
    o = jnp.einsum('bhqk,bhkd->bhqd', p, v)
    return o.transpose(0, 2, 1, 3).reshape(B, n, H * hd)


def _merge(o_a, o_b, gate_a, gate_b, w_pa, w_pb, w_out):
    y = jax.nn.sigmoid(gate_a) * (o_a @ w_pa) + jax.nn.sigmoid(gate_b) * (o_b @ w_pb)
    return y @ w_out


def _split_in(p):
    sizes = [3 * DN_W, DN_W, 2 * DN_HEADS, 2 * DN_HEADS, 3 * NA_W, D_MODEL]
    return jnp.split(p, np.cumsum(sizes), axis=-1)


def _hybrid_mixer(hx, hc, w_in, conv_w, a_log, dt_bias, dn_norm_w, q_w, k_w, rpb,
                  w_pa, w_pb, w_out, with_ctx):
    dqkv_x, z_x, a_x, b_x, nqkv_x, ga_x, gb_x = _split_in(hx @ w_in)
    dqkv_c, z_c, a_c, b_c, nqkv_c, ga_c, gb_c = _split_in(hc @ w_in)
    dn_x = _delta_inputs(dqkv_x, a_x, b_x, conv_w, a_log, dt_bias, True)
    dn_c = _delta_inputs(dqkv_c, a_c, b_c, conv_w, a_log, dt_bias, False)
    o_dn_c, o_dn_x = _bidir_delta(*dn_c, *dn_x)
    qx, kx, vx = _na_inputs(nqkv_x, q_w, k_w)
    qc, kc, vc = _na_inputs(nqkv_c, q_w, k_w)
    o_na_x = _neighbourhood_attention(qx, kx, vx, kc, vc, rpb)
    y_x = _merge(_gated_head_norm(o_dn_x, z_x, dn_norm_w), o_na_x, ga_x, gb_x, w_pa, w_pb, w_out)
    if not with_ctx:
        return y_x, None
    o_na_c = _context_attention(qc, kc, vc)
    y_c = _merge(_gated_head_norm(o_dn_c, z_c, dn_norm_w), o_na_c, ga_c, gb_c, w_pa, w_pb, w_out)
    return y_x, y_c


def _peer_ffn(h, w_q, sub_keys, u_emb, v_emb):
    T, D = h.shape

    def block(hb):
        tb = hb.shape[0]
        q = (hb @ w_q).reshape(tb, PEER_HEADS, 2, PEER_DK)
        s = jnp.einsum('thpd,hpnd->thpn', q, sub_keys).astype(jnp.float32)
        top_s, top_i = lax.top_k(s, PEER_TOPK)
        cand_s = top_s[:, :, 0, :, None] + top_s[:, :, 1, None, :]
        cand_i = top_i[:, :, 0, :, None] * N_KEYS + top_i[:, :, 1, None, :]
        best_s, pos = lax.top_k(cand_s.reshape(tb, PEER_HEADS, PEER_TOPK * PEER_TOPK), PEER_TOPK)
        idx = jnp.take_along_axis(cand_i.reshape(tb, PEER_HEADS, PEER_TOPK * PEER_TOPK), pos, axis=-1)
        gate = jax.nn.softmax(best_s, axis=-1)
        act = jax.nn.gelu(jnp.einsum('td,thkd->thk', hb, u_emb[idx]))
        return jnp.einsum('thk,thkd->td', (gate * act).astype(hb.dtype), v_emb[idx])

    out = lax.map(block, h.reshape(T // PEER_BLOCK, PEER_BLOCK, D))
    return out.reshape(T, D)


def setup_inputs(seed: int = 0) -> dict:
    key = jax.random.key(seed)
    ks = jax.random.split(key, 24)
    f32 = jnp.float32

    def nrm(k, shape, s):
        return s * jax.random.normal(k, shape, f32)

    def gain(k, shape):
        return 1.0 + nrm(k, shape, 0.05)

    D = D_MODEL
    a_log = jnp.log(jax.random.uniform(ks[9], (DEPTH, 2, DN_HEADS), f32, A_MIN, A_MAX))
    dt = jnp.exp(jax.random.uniform(ks[10], (DEPTH, 2, DN_HEADS), f32,
                                    float(np.log(DT_MIN)), float(np.log(DT_MAX))))
    dt_bias = dt + jnp.log(-jnp.expm1(-dt))
    return {
        'x': nrm(ks[0], (BATCH, SEQ, D), 1.0),
        'c': nrm(ks[1], (BATCH, D), 1.0),
        'ctx': nrm(ks[2], (BATCH, CTX_LEN, D), 1.0),
        'c_ctx': nrm(ks[3], (D,), 1.0),
        'ada_w': nrm(ks[4], (DEPTH, D, N_MOD * D), 0.5 * D ** -0.5),
        'ada_b': nrm(ks[5], (DEPTH, N_MOD * D), 0.01),
        'norm1_w': gain(ks[6], (DEPTH, D)),
        'norm2_w': gain(ks[7], (DEPTH, D)),
        'w_in': nrm(ks[8], (DEPTH, D, IN_W), D ** -0.5),
        'dn_conv_w': nrm(ks[11], (DEPTH, CONV_K, 3 * DN_W), CONV_K ** -0.5),
        'dn_a_log': a_log,
        'dn_dt_bias': dt_bias,
        'dn_norm_w': gain(ks[12], (DEPTH, DN_DK)),
        'na_qnorm_w': gain(ks[13], (DEPTH, NA_DH)),
        'na_knorm_w': gain(ks[14], (DEPTH, NA_DH)),
        'na_rpb': nrm(ks[15], (DEPTH, NA_HEADS, 2 * WIN_R - 1, 2 * WIN_C - 1), 0.1),
        'w_pa': nrm(ks[16], (DEPTH, DN_W, D), DN_W ** -0.5),
        'w_pb': nrm(ks[17], (DEPTH, NA_W, D), NA_W ** -0.5),
        'w_out': nrm(ks[18], (DEPTH, D, D), D ** -0.5),
        'peer_wq': nrm(ks[19], (DEPTH, D, PEER_HEADS * 2 * PEER_DK), D ** -0.5),
        'peer_keys': nrm(ks[20], (DEPTH, PEER_HEADS, 2, N_KEYS, PEER_DK), PEER_DK ** -0.5),
        'peer_u': nrm(ks[21], (DEPTH, N_EXPERTS, D), D ** -0.5),
        'peer_v': nrm(ks[22], (DEPTH, N_EXPERTS, D), PEER_HEADS ** -0.5),
    }


def reference(x, c, ctx, c_ctx, ada_w, ada_b, norm1_w, norm2_w, w_in, dn_conv_w, dn_a_log,
              dn_dt_bias, dn_norm_w, na_qnorm_w, na_knorm_w, na_rpb, w_pa, w_pb, w_out,
              peer_wq, peer_keys, peer_u, peer_v):
    B, L, D = x.shape
    n_ctx = ctx.shape[1]
    for l in range(DEPTH):
        last = l == DEPTH - 1
        mod_x = jax.nn.silu(c) @ ada_w[l] + ada_b[l]
        mod_c = jax.nn.silu(c_ctx) @ ada_w[l] + ada_b[l]
        sh1x, sc1x, g1x, sh2x, sc2x, g2x = (t[:, None, :] for t in jnp.split(mod_x, N_MOD, axis=-1))
        sh1c, sc1c, g1c, sh2c, sc2c, g2c = jnp.split(mod_c, N_MOD, axis=-1)
        hx = _modulate(_rmsnorm(x, norm1_w[l]), sh1x, sc1x)
        hc = _modulate(_rmsnorm(ctx, norm1_w[l]), sh1c, sc1c)
        y_x, y_c = _hybrid_mixer(hx, hc, w_in[l], dn_conv_w[l], dn_a_log[l], dn_dt_bias[l],
                                 dn_norm_w[l], na_qnorm_w[l], na_knorm_w[l], na_rpb[l],
                                 w_pa[l], w_pb[l], w_out[l], not last)
        x = x + g1x * y_x
        hx = _modulate(_rmsnorm(x, norm2_w[l]), sh2x, sc2x)
        if last:
            f = _peer_ffn(hx.reshape(B * L, D), peer_wq[l], peer_keys[l], peer_u[l], peer_v[l])
            x = x + g2x * f.reshape(B, L, D)
        else:
            ctx = ctx + g1c * y_c
            hc = _modulate(_rmsnorm(ctx, norm2_w[l]), sh2c, sc2c)
            tokens = jnp.concatenate([hx.reshape(B * L, D), hc.reshape(B * n_ctx, D)], axis=0)
            f = _peer_ffn(tokens, peer_wq[l], peer_keys[l], peer_u[l], peer_v[l])
            x = x + g2x * f[:B * L].reshape(B, L, D)
            ctx = ctx + g2c * f[B * L:].reshape(B, n_ctx, D)
    return x
```

```python
import functools

import numpy as np
import jax
import jax.numpy as jnp
from jax import lax
from jax.experimental import pallas as pl
from jax.experimental.pallas import tpu as pltpu

F32 = jnp.float32
BF16 = jnp.bfloat16

GRID_W = 64
DN_HEADS = 8
DN_DK = 64
DN_W = DN_HEADS * DN_DK
DN_CHUNK = 64
CONV_K = 5
NA_HEADS = 8
NA_DH = 64
NA_W = NA_HEADS * NA_DH
WIN_R = 8
WIN_C = 16
ROPE_BASE = 10000.0
PEER_HEADS = 8
N_KEYS = 128
PEER_DK = 128
PEER_TOPK = 16
N_MOD = 6
EPS = 1e-6
NEG_INF = -1e30

LANES = 128
TOK_TILE = 256
DN_GROUP = 4
DN_GW = DN_GROUP * DN_DK
AB_W = 128
IN_W_PAD = 3 * DN_W + DN_W + AB_W + 3 * NA_W + 2 * 1024
PEER_TB = 512
PEER_EB = 1024
VMEM_LIMIT = 56 * 1024 * 1024


def _cp(sem, vmem=VMEM_LIMIT):
    return pltpu.CompilerParams(dimension_semantics=sem, vmem_limit_bytes=vmem)


def _dot(a, b):
    return jnp.dot(a, b, preferred_element_type=F32)


def _dot_nt(a, b):
    return lax.dot_general(a, b, (((1,), (1,)), ((), ())), preferred_element_type=F32)


def _dot_tn(a, b):
    return lax.dot_general(a, b, (((0,), (0,)), ((), ())), preferred_element_type=F32)


def _split2(x):
    hi = x.astype(BF16)
    lo = (x - hi.astype(F32)).astype(BF16)
    return hi, lo


def _dot_exact_lhs(m_bf16, x):
    hi, lo = _split2(x)
    return _dot(m_bf16, hi) + _dot(m_bf16, lo)


def _dot_exact_rhs(x, m_bf16):
    hi, lo = _split2(x)
    return _dot(hi, m_bf16) + _dot(lo, m_bf16)


def _silu(x):
    return x * jax.nn.sigmoid(x)


def _softplus(x):
    return jnp.maximum(x, 0.0) + jnp.log1p(jnp.exp(-jnp.abs(x)))


def _log2(n):
    assert n & (n - 1) == 0
    return n.bit_length() - 1


def _same_group_ones(n, group):
    r = lax.broadcasted_iota(jnp.int32, (n, n), 0) >> _log2(group)
    c = lax.broadcasted_iota(jnp.int32, (n, n), 1) >> _log2(group)
    return jnp.where(r == c, 1.0, 0.0).astype(BF16)


def _mod_kernel(c_ref, w_ref, b_ref, o_ref):
    s = _silu(c_ref[...])
    o_ref[0] = _dot(s.astype(BF16), w_ref[0].astype(BF16)) + b_ref[0]


def _modulation(cc, ada_w, ada_b):
    depth, d, _ = ada_w.shape
    rows = cc.shape[0]
    return pl.pallas_call(
        _mod_kernel,
        out_shape=jax.ShapeDtypeStruct((depth, rows, N_MOD * d), F32),
        grid=(depth, N_MOD),
        in_specs=[
            pl.BlockSpec((rows, d), lambda l, j: (0, 0)),
            pl.BlockSpec((1, d, d), lambda l, j: (l, 0, j)),
            pl.BlockSpec((1, 1, d), lambda l, j: (l, 0, j)),
        ],
        out_specs=pl.BlockSpec((1, rows, d), lambda l, j: (l, 0, j)),
        compiler_params=_cp(("arbitrary", "arbitrary")),
        name="adaln_modulation",
    )(cc, ada_w, ada_b.reshape(depth, 1, N_MOD * d))


_IN_SPLITS = (3 * DN_W, DN_W, AB_W, 3 * NA_W, 1024, 1024)


def _inproj_kernel(x_ref, mod_ref, nw_ref, w_ref, *out_refs):
    x = x_ref[0]
    ms = jnp.mean(x * x, axis=-1, keepdims=True)
    y = x * lax.rsqrt(ms + EPS) * nw_ref[...]
    h = y * (1.0 + mod_ref[0, 0, 1:2, :]) + mod_ref[0, 0, 0:1, :]
    hb = h.astype(BF16)
    off = 0
    for o_ref, width in zip(out_refs, _IN_SPLITS):
        o_ref[0] = _dot(hb, w_ref[:, off:off + width])
        off += width


def _in_projection(stream, modsel, norm_w, w_in_p):
    b, s, d = stream.shape
    tm = TOK_TILE
    tok = lambda width: pl.BlockSpec((1, tm, width), lambda i, t: (i, t, 0))
    return pl.pallas_call(
        _inproj_kernel,
        out_shape=[jax.ShapeDtypeStruct((b, s, width), F32) for width in _IN_SPLITS],
        grid=(b, s // tm),
        in_specs=[
            tok(d),
            pl.BlockSpec((1, 1, 8, d), lambda i, t: (i, jnp.minimum(t, 1), 0, 0)),
            pl.BlockSpec((1, d), lambda i, t: (0, 0)),
            pl.BlockSpec((d, IN_W_PAD), lambda i, t: (0, 0)),
        ],
        out_specs=[tok(width) for width in _IN_SPLITS],
        compiler_params=_cp(("arbitrary", "arbitrary")),
        name="norm_in_projection",
    )(stream, modsel, norm_w.reshape(1, d), w_in_p)


def _dnprep_kernel(x_ref, cw_ref, cos_ref, sin_ref, o_ref, *, n_ctx):
    j = pl.program_id(1)
    x = x_ref[0]
    s = x.shape[0]
    t = lax.broadcasted_iota(jnp.int32, x.shape, 0)
    in_ctx = t < n_ctx
    pos = jnp.where(in_ctx, t, t - n_ctx)
    seg_len = jnp.where(in_ctx, n_ctx, s - n_ctx)
    y = jnp.zeros_like(x)
    for tap in range(CONV_K):
        d = tap - CONV_K // 2
        xs = x if d == 0 else pltpu.roll(x, (s - d) % s, 0)
        ok = (pos + d >= 0) & (pos + d < seg_len)
        y = y + jnp.where(ok, xs, 0.0) * cw_ref[tap:tap + 1, :]
    y = _silu(y)

    @pl.when(j < 2 * (DN_W // LANES))
    def _():
        ones = _same_group_ones(LANES, DN_DK)
        ss = _dot_exact_rhs(y * y, ones)
        yn = y * lax.rsqrt(ss + EPS)
        lane = lax.broadcasted_iota(jnp.int32, x.shape, 1)
        first_half = (lane & (DN_DK - 1)) < DN_DK // 2
        partner = jnp.where(first_half,
                            pltpu.roll(yn, LANES - DN_DK // 2, 1),
                            pltpu.roll(yn, DN_DK // 2, 1))
        out = yn * cos_ref[...] + partner * sin_ref[...]
        qscale = jnp.where(j < DN_W // LANES, DN_DK ** -0.5, 1.0)
        o_ref[0] = out * qscale

    @pl.when(j >= 2 * (DN_W // LANES))
    def _():
        o_ref[0] = y


def _dn_prep(dqkv, conv_w, cos_t, sin_t, n_ctx):
    b, s, w = dqkv.shape
    return pl.pallas_call(
        functools.partial(_dnprep_kernel, n_ctx=n_ctx),
        out_shape=jax.ShapeDtypeStruct((b, s, w), F32),
        grid=(b, w // LANES),
        in_specs=[
            pl.BlockSpec((1, s, LANES), lambda i, j: (i, 0, j)),
            pl.BlockSpec((CONV_K, LANES), lambda i, j: (0, j)),
            pl.BlockSpec((s, LANES), lambda i, j: (0, 0)),
            pl.BlockSpec((s, LANES), lambda i, j: (0, 0)),
        ],
        out_specs=pl.BlockSpec((1, s, LANES), lambda i, j: (i, 0, j)),
        compiler_params=_cp(("arbitrary", "arbitrary")),
        name="deltanet_prep",
    )(dqkv, conv_w, cos_t, sin_t)


def _tile_rows_bd(x, bd_mask):
    return jnp.where(bd_mask, jnp.concatenate([x] * DN_GROUP, axis=0), 0.0)


def _dnscan_kernel(q_ref, k_ref, v_ref, ab_ref, ea_ref, eb_ref, nega_ref, dtb_ref, o_ref, s_ref):
    c = DN_CHUNK
    gw = DN_GW
    sign = 1 - 2 * pl.program_id(1)

    @pl.when(pl.program_id(2) == 0)
    def _():
        s_ref[...] = jnp.zeros_like(s_ref)

    ab = ab_ref[0]
    a_exp = _dot_exact_rhs(ab, ea_ref[0])
    b_exp = _dot_exact_rhs(ab, eb_ref[0])
    g_all = nega_ref[0] * _softplus(a_exp + dtb_ref[0])
    beta_all = jax.nn.sigmoid(b_exp)

    row = lax.broadcasted_iota(jnp.int32, (c, gw), 0)
    col = lax.broadcasted_iota(jnp.int32, (c, gw), 1) & (c - 1)
    lag = (row - col) * sign
    incl = lag >= 0
    strict = lag > 0
    eye = jnp.where(lag == 0, 1.0, 0.0)
    rr = lax.broadcasted_iota(jnp.int32, (c, c), 0)
    cc = lax.broadcasted_iota(jnp.int32, (c, c), 1)
    tri = jnp.where((rr - cc) * sign >= 0, 1.0, 0.0).astype(BF16)
    tri_t = jnp.where((col - row) * sign >= 0, 1.0, 0.0)
    ones8 = jnp.ones((8, c), BF16)
    bd_r = lax.broadcasted_iota(jnp.int32, (gw, gw), 0) >> _log2(c)
    bd_c = lax.broadcasted_iota(jnp.int32, (gw, gw), 1) >> _log2(c)
    bd_mask = bd_r == bd_c

    for gi in range(DN_HEADS // DN_GROUP):
        sl = slice(gi * gw, (gi + 1) * gw)
        q = q_ref[0, :, sl]
        k = k_ref[0, :, sl]
        v = v_ref[0, :, sl]
        g = g_all[:, sl]
        beta = beta_all[:, sl]

        g_row = _dot_exact_lhs(tri, g)
        g_col = _dot_exact_lhs(ones8, g * tri_t)[0:1]
        g_tot = _dot_exact_lhs(ones8, g)[0:1]
        decay = jnp.where(incl, jnp.exp(jnp.where(incl, g_row - g_col, 0.0)), 0.0)

        kb = k * beta
        k_bd = _tile_rows_bd(k, bd_mask).astype(BF16)
        kq = _dot_nt(jnp.concatenate([kb, q], axis=0).astype(BF16), k_bd)
        a_mat = jnp.where(strict, kq[:c] * decay, 0.0)
        intra = jnp.where(incl, kq[c:] * decay, 0.0)

        t_mat = eye - a_mat
        x = _dot(a_mat.astype(BF16), _tile_rows_bd(a_mat, bd_mask).astype(BF16))
        for _ in range(4):
            x_bd = _tile_rows_bd(x, bd_mask).astype(BF16)
            r = _dot(jnp.concatenate([x, t_mat], axis=0).astype(BF16), x_bd)
            x = r[:c]
            t_mat = t_mat + r[c:]
        t_mat = t_mat + _dot(t_mat.astype(BF16), _tile_rows_bd(x, bd_mask).astype(BF16))
        tb = t_mat.astype(BF16)

        e_row = jnp.exp(g_row)
        w_val = _dot(tb, _tile_rows_bd(v * beta, bd_mask).astype(BF16))
        u_key = _dot(tb, _tile_rows_bd(kb * e_row, bd_mask).astype(BF16))
        q_dec = q * e_row
        k_dec = k * jnp.exp(g_tot - g_row)

        state = s_ref[gi]
        r2 = _dot(jnp.concatenate([u_key, q_dec], axis=0).astype(BF16), state.astype(BF16))
        v_new = w_val - r2[:c]
        o = r2[c:] + _dot(intra.astype(BF16), _tile_rows_bd(v_new, bd_mask).astype(BF16))
        upd = _dot_tn(k_dec.astype(BF16), v_new.astype(BF16))
        s_ref[gi] = state * jnp.exp(g_tot) + jnp.where(bd_mask, upd, 0.0)
        o_ref[0, 0, :, sl] = o


def _dn_scan(qkvn, ab, e_a, e_b, neg_a, dt_b, n_ctx):
    b, s, _ = qkvn.shape
    c = DN_CHUNK
    n_chunks = s // c
    ctx_chunks = n_ctx // c
    nq = DN_W // DN_W

    def chunk(d, i):
        bwd = jnp.where(i < ctx_chunks, ctx_chunks - 1 - i, n_chunks - 1 + ctx_chunks - i)
        return jnp.where(d == 0, i, bwd)

    def tok(part):
        return pl.BlockSpec((1, c, DN_W), lambda i, d, t: (i, chunk(d, t), part))

    per_dir = lambda shape: pl.BlockSpec((1,) + shape, lambda i, d, t: (d,) + (0,) * len(shape))
    return pl.pallas_call(
        _dnscan_kernel,
        out_shape=jax.ShapeDtypeStruct((b, 2, s, DN_W), F32),
        grid=(b, 2, n_chunks),
        in_specs=[
            tok(0), tok(nq), tok(2 * nq),
            pl.BlockSpec((1, c, AB_W), lambda i, d, t: (i, chunk(d, t), 0)),
            per_dir((AB_W, DN_W)), per_dir((AB_W, DN_W)),
            per_dir((1, DN_W)), per_dir((1, DN_W)),
        ],
        out_specs=pl.BlockSpec((1, 1, c, DN_W), lambda i, d, t: (i, d, chunk(d, t), 0)),
        scratch_shapes=[pltpu.VMEM((DN_HEADS // DN_GROUP, DN_GW, DN_GW), F32)],
        compiler_params=_cp(("arbitrary", "arbitrary", "arbitrary")),
        name="deltanet_scan",
    )(qkvn, qkvn, qkvn, ab, e_a, e_b, neg_a, dt_b)


def _na_kernel(q_ref, k_ref, v_ref, qw_ref, kw_ref, bias_ref, o_ref, qm_ref, kn_ref, vm_ref,
               *, n_ctx, rows):
    lane = lax.broadcasted_iota(jnp.int32, (1, LANES), 1)
    head0 = lane < NA_DH
    ones = _same_group_ones(LANES, NA_DH)

    def head_norm(x, w):
        ms = _dot_exact_rhs(x * x, ones) * (1.0 / NA_DH)
        return x * lax.rsqrt(ms + EPS) * w

    q = head_norm(q_ref[0], qw_ref[...]) * NA_DH ** -0.5
    qm_ref[0] = jnp.where(head0, q, 0.0).astype(BF16)
    qm_ref[1] = jnp.where(head0, 0.0, q).astype(BF16)
    kn_ref[...] = head_norm(k_ref[0], kw_ref[...]).astype(BF16)
    v = v_ref[0]
    vm_ref[0] = jnp.where(head0, v, 0.0).astype(BF16)
    vm_ref[1] = jnp.where(head0, 0.0, v).astype(BF16)

    k_ctx = kn_ref[0:n_ctx, :]
    wr = WIN_R
    band = wr * GRID_W

    def softmax_pv(parts):
        m = functools.reduce(jnp.maximum, [jnp.max(sc, axis=-1, keepdims=True) for sc, _ in parts])
        ps = [jnp.exp(sc - m) for sc, _ in parts]
        l = functools.reduce(jnp.add, [jnp.sum(p, axis=-1, keepdims=True) for p in ps])
        o = functools.reduce(jnp.add, [_dot(p.astype(BF16), val) for p, (_, val) in zip(ps, parts)])
        return o / l

    for blk in range(n_ctx // GRID_W):
        acc = jnp.zeros((GRID_W, LANES), F32)
        for h in range(2):
            qh = qm_ref[h, blk * GRID_W:(blk + 1) * GRID_W, :]
            acc = acc + softmax_pv([(_dot_nt(qh, k_ctx), vm_ref[h, 0:n_ctx, :])])
        o_ref[0, blk * GRID_W:(blk + 1) * GRID_W, :] = acc

    def row_body(r, carry):
        rs = jnp.clip(r - wr // 2, 0, rows - wr)
        var = rs - r + (WIN_R - 1)
        q0 = pl.multiple_of(n_ctx + r * GRID_W, GRID_W)
        k0 = pl.multiple_of(n_ctx + rs * GRID_W, GRID_W)
        k_band = kn_ref[pl.ds(k0, band), :]
        acc = jnp.zeros((GRID_W, LANES), F32)
        for h in range(2):
            qh = qm_ref[h, pl.ds(q0, GRID_W), :]
            s_loc = _dot_nt(qh, k_band) + bias_ref[h, var]
            s_ctx = _dot_nt(qh, k_ctx)
            acc = acc + softmax_pv([(s_loc, vm_ref[h, pl.ds(k0, band), :]),
                                    (s_ctx, vm_ref[h, 0:n_ctx, :])])
        o_ref[0, pl.ds(q0, GRID_W), :] = acc
        return carry

    lax.fori_loop(0, rows, row_body, 0)


def _na_attention(nqkv, q_w, k_w, bias_tab, n_ctx):
    b, s, _ = nqkv.shape
    rows = (s - n_ctx) // GRID_W
    pairs = NA_W // LANES
    tok = lambda part: pl.BlockSpec((1, s, LANES), lambda i, p: (i, 0, part * pairs + p))
    return pl.pallas_call(
        functools.partial(_na_kernel, n_ctx=n_ctx, rows=rows),
        out_shape=jax.ShapeDtypeStruct((b, s, NA_W), F32),
        grid=(b, pairs),
        in_specs=[
            tok(0), tok(1), tok(2),
            pl.BlockSpec((1, LANES), lambda i, p: (0, 0)),
            pl.BlockSpec((1, LANES), lambda i, p: (0, 0)),
            pl.BlockSpec((2, WIN_R, GRID_W, WIN_R * GRID_W), lambda i, p: (p, 0, 0, 0)),
        ],
        out_specs=pl.BlockSpec((1, s, LANES), lambda i, p: (i, 0, p)),
        scratch_shapes=[pltpu.VMEM((2, s, LANES), BF16), pltpu.VMEM((s, LANES), BF16),
                        pltpu.VMEM((2, s, LANES), BF16)],
        compiler_params=_cp(("arbitrary", "arbitrary")),
        name="neighbourhood_attention",
    )(nqkv, nqkv, nqkv, jnp.tile(q_w, 2).reshape(1, LANES), jnp.tile(k_w, 2).reshape(1, LANES),
      bias_tab)


def _merge_kernel(x_ref, odn_ref, z_ref, ona_ref, ga_ref, gb_ref, mod_ref, dnw_ref, n2w_ref,
                  ones_ref, wpa_ref, wpb_ref, wout_ref, wq_ref, xo_ref, h2_ref, pq_ref):
    o_dn = odn_ref[0, 0] + odn_ref[0, 1]
    parts = []
    for j in range(DN_W // LANES):
        oj = o_dn[:, j * LANES:(j + 1) * LANES]
        ms = _dot_exact_rhs(oj * oj, ones_ref[...]) * (1.0 / DN_DK)
        parts.append(oj * lax.rsqrt(ms + EPS))
    o_a = jnp.concatenate(parts, axis=1) * dnw_ref[...] * _silu(z_ref[0])
    y = (jax.nn.sigmoid(ga_ref[0]) * _dot(o_a.astype(BF16), wpa_ref[...])
         + jax.nn.sigmoid(gb_ref[0]) * _dot(ona_ref[0].astype(BF16), wpb_ref[...]))
    x = x_ref[0] + mod_ref[0, 0, 2:3, :] * _dot(y.astype(BF16), wout_ref[...])
    xo_ref[0] = x
    ms = jnp.mean(x * x, axis=-1, keepdims=True)
    h2 = (x * lax.rsqrt(ms + EPS) * n2w_ref[...]) * (1.0 + mod_ref[0, 0, 4:5, :]) + mod_ref[0, 0, 3:4, :]
    h2b = h2.astype(BF16)
    h2_ref[0] = h2b
    pq_ref[0] = _dot(h2b, wq_ref[...]).astype(BF16)


def _merge(stream, o_dn, z, o_na, ga, gb, modsel, dn_norm_w, norm2_w, w_pa, w_pb, w_out, w_q):
    b, s, d = stream.shape
    tm = TOK_TILE
    qw = w_q.shape[1]
    tok = lambda width: pl.BlockSpec((1, tm, width), lambda i, t: (i, t, 0))
    full = lambda arr: pl.BlockSpec(arr.shape, lambda i, t: (0,) * arr.ndim)
    dnw = jnp.tile(dn_norm_w, DN_HEADS).reshape(1, DN_W)
    n2w = norm2_w.reshape(1, d)
    ones = jnp.asarray(np.kron(np.eye(LANES // DN_DK), np.ones((DN_DK, DN_DK))), BF16)
    return pl.pallas_call(
        _merge_kernel,
        out_shape=[jax.ShapeDtypeStruct((b, s, d), F32), jax.ShapeDtypeStruct((b, s, d), BF16),
                   jax.ShapeDtypeStruct((b, s, qw), BF16)],
        grid=(b, s // tm),
        in_specs=[
            tok(d),
            pl.BlockSpec((1, 2, tm, DN_W), lambda i, t: (i, 0, t, 0)),
            tok(DN_W), tok(NA_W), tok(d), tok(d),
            pl.BlockSpec((1, 1, 8, d), lambda i, t: (i, jnp.minimum(t, 1), 0, 0)),
            full(dnw), full(n2w), full(ones), full(w_pa), full(w_pb), full(w_out), full(w_q),
        ],
        out_specs=[tok(d), tok(d), tok(qw)],
        compiler_params=_cp(("arbitrary", "arbitrary")),
        name="merge_projection",
    )(stream, o_dn, z, o_na, ga, gb, modsel, dnw, n2w, ones, w_pa, w_pb, w_out, w_q)


def _gelu_tanh(x):
    return 0.5 * x * (1.0 + jnp.tanh(0.7978845608028654 * (x + 0.044715 * (x * x * x))))


def _peer_topk_chunk(s1, s2, t1_ref, t2_ref):
    neg = jnp.float32(-jnp.inf)
    for t_ref, s in ((t1_ref, s1), (t2_ref, s2)):
        cur = s
        for kk in range(PEER_TOPK):
            m = jnp.max(cur, axis=0, keepdims=True)
            t_ref[kk:kk + 1, :] = m
            if kk + 1 < PEER_TOPK:
                cur = jnp.where(cur == m, neg, cur)
    t1 = t1_ref[...]
    t2 = t2_ref[...]
    sub = lax.broadcasted_iota(jnp.int32, (8, LANES), 0)
    cands = [t1[0:1] + t2[0:8], t1[0:1] + t2[8:16], t1[1:2] + t2[0:8]]
    for i in range(2, 8):
        cands.append(jnp.where(sub < PEER_TOPK // (i + 1), t1[i:i + 1] + t2[0:8], neg))
    cands.append(t1[8:16] + t2[0:1])
    cur = cands
    tau = None
    for kk in range(PEER_TOPK):
        m = jnp.max(functools.reduce(jnp.maximum, cur), axis=0, keepdims=True)
        if kk + 1 < PEER_TOPK:
            cur = [jnp.where(cd == m, neg, cd) for cd in cur]
        else:
            tau = m
    m1 = t1[0:1]
    m2 = t2[0:1]
    top = m1 + m2
    z = functools.reduce(jnp.add, [
        jnp.sum(jnp.where(cd >= tau, jnp.exp(cd - top), 0.0), axis=0, keepdims=True) for cd in cands])
    return tau, m1, m2, z


def _peer_kernel(x_ref, h2_ref, pq_ref, g2_ref, keys_ref, u_ref, vt_ref, o_ref,
                 s_ref, e1_ref, e2_ref, tau_ref, t1_ref, t2_ref, g_ref, acc_ref, *, sub_tile):
    e = pl.program_id(1)
    tb = h2_ref.shape[0]
    n_chunk = tb // LANES
    eb = u_ref.shape[0]

    @pl.when(e == 0)
    def _():
        acc_ref[...] = jnp.zeros_like(acc_ref)
        for hp in range(2 * PEER_HEADS):
            st = _dot_nt(keys_ref[hp], pq_ref[:, hp * PEER_DK:(hp + 1) * PEER_DK])
            for ci in range(n_chunk):
                s_ref[hp, ci] = st[:, ci * LANES:(ci + 1) * LANES]

        def topk_body(idx, carry):
            h = idx // n_chunk
            ci = idx % n_chunk
            s1 = s_ref[2 * h, ci]
            s2 = s_ref[2 * h + 1, ci]
            tau, m1, m2, z = _peer_topk_chunk(s1, s2, t1_ref, t2_ref)
            e1_ref[h, ci] = jnp.exp(s1 - m1) / z
            e2_ref[h, ci] = jnp.exp(s2 - m2)
            tau_ref[h, ci] = jnp.broadcast_to(tau, (8, LANES))
            return carry

        lax.fori_loop(0, PEER_HEADS * n_chunk, topk_body, 0)

    act = _gelu_tanh(_dot_nt(u_ref[...], h2_ref[...]))

    def a_body(al, carry):
        a = e * (eb // N_KEYS) + al
        r0 = pl.multiple_of(al * N_KEYS, N_KEYS)
        for ci in range(n_chunk):
            w = jnp.zeros((N_KEYS, LANES), F32)
            for h in range(PEER_HEADS):
                s1row = s_ref[2 * h, ci, pl.ds(a, 1), :]
                e1row = e1_ref[h, ci, pl.ds(a, 1), :]
                tsum = s1row + s_ref[2 * h + 1, ci]
                w = w + jnp.where(tsum >= tau_ref[h, ci, 0:1, :], e1row * e2_ref[h, ci], 0.0)
            g_ref[pl.ds(r0, N_KEYS), ci * LANES:(ci + 1) * LANES] = w
        return carry

    lax.fori_loop(0, eb // N_KEYS, a_body, 0)
    acc_ref[...] += _dot(vt_ref[...], (g_ref[...] * act).astype(BF16))

    @pl.when(e == pl.num_programs(1) - 1)
    def _():
        f = acc_ref[...].T
        for si in range(tb // sub_tile):
            rows = slice(si * sub_tile, (si + 1) * sub_tile)
            o_ref[rows, :] = x_ref[rows, :] + g2_ref[si] * f[rows, :]


def _peer(x_flat, h2_flat, pq_flat, g2_sub, keys_b, u_b, vt_b):
    t, d = x_flat.shape
    tb = PEER_TB
    eb = PEER_EB
    n_exp = u_b.shape[0]
    n_chunk = tb // LANES
    n_sub = tb // TOK_TILE
    return pl.pallas_call(
        functools.partial(_peer_kernel, sub_tile=TOK_TILE),
        out_shape=jax.ShapeDtypeStruct((t, d), F32),
        grid=(t // tb, n_exp // eb),
        in_specs=[
            pl.BlockSpec((tb, d), lambda i, e: (i, 0)),
            pl.BlockSpec((tb, d), lambda i, e: (i, 0)),
            pl.BlockSpec((tb, pq_flat.shape[1]), lambda i, e: (i, 0)),
            pl.BlockSpec((n_sub, 1, d), lambda i, e: (i, 0, 0)),
            pl.BlockSpec(keys_b.shape, lambda i, e: (0, 0, 0)),
            pl.BlockSpec((eb, d), lambda i, e: (e, 0)),
            pl.BlockSpec((d, eb), lambda i, e: (0, e)),
        ],
        out_specs=pl.BlockSpec((tb, d), lambda i, e: (i, 0)),
        scratch_shapes=[
            pltpu.VMEM((2 * PEER_HEADS, n_chunk, N_KEYS, LANES), F32),
            pltpu.VMEM((PEER_HEADS, n_chunk, N_KEYS, LANES), F32),
            pltpu.VMEM((PEER_HEADS, n_chunk, N_KEYS, LANES), F32),
            pltpu.VMEM((PEER_HEADS, n_chunk, 8, LANES), F32),
            pltpu.VMEM((PEER_TOPK, LANES), F32),
            pltpu.VMEM((PEER_TOPK, LANES), F32),
            pltpu.VMEM((eb, tb), F32),
            pltpu.VMEM((d, tb), F32),
        ],
        compiler_params=_cp(("arbitrary", "arbitrary")),
        name="peer_dense",
    )(x_flat, h2_flat, pq_flat, g2_sub, keys_b, u_b, vt_b)


def _static_tables(n_ctx, n_lat):
    n_freq = DN_DK // 4
    t = np.arange(n_lat)
    row = (t // GRID_W).astype(np.float32)
    col = (t % GRID_W).astype(np.float32)
    inv_freq = (np.float32(ROPE_BASE) ** (-np.arange(n_freq, dtype=np.float32) / n_freq)).astype(np.float32)
    ang = np.concatenate([row[:, None] * inv_freq, col[:, None] * inv_freq], axis=-1)
    cos_h = np.concatenate([np.cos(ang), np.cos(ang)], axis=-1)
    sin_h = np.concatenate([-np.sin(ang), np.sin(ang)], axis=-1)
    cos_t = np.concatenate([np.ones((n_ctx, DN_DK), np.float32), cos_h], axis=0)
    sin_t = np.concatenate([np.zeros((n_ctx, DN_DK), np.float32), sin_h], axis=0)
    cos_t = jnp.asarray(np.tile(cos_t, (1, LANES // DN_DK)), F32)
    sin_t = jnp.asarray(np.tile(sin_t, (1, LANES // DN_DK)), F32)
    sel = np.zeros((2, 2, AB_W, DN_W), np.float32)
    for which in range(2):
        for dr in range(2):
            for h in range(DN_HEADS):
                sel[which, dr, which * 2 * DN_HEADS + dr * DN_HEADS + h, h * DN_DK:(h + 1) * DN_DK] = 1.0
    return cos_t, sin_t, jnp.asarray(sel[0], BF16), jnp.asarray(sel[1], BF16)


def _na_bias_table(rpb):
    c = np.arange(GRID_W)
    kc = np.arange(GRID_W)
    cs = np.clip(c - WIN_C // 2, 0, GRID_W - WIN_C)
    in_win = (kc[None, :] >= cs[:, None]) & (kc[None, :] < cs[:, None] + WIN_C)
    dc = np.clip(kc[None, :] - c[:, None] + (WIN_C - 1), 0, 2 * WIN_C - 2)
    var = np.arange(WIN_R)
    jj = np.arange(WIN_R)
    dr = var[:, None] + jj[None, :]
    tab = rpb[:, dr][:, :, :, dc]
    tab = jnp.where(in_win[None, None, None], tab, NEG_INF)
    tab = tab.transpose(0, 1, 3, 2, 4)
    return tab.reshape(rpb.shape[0], WIN_R, GRID_W, WIN_R * GRID_W)


def kernel(x, c, ctx, c_ctx, ada_w, ada_b, norm1_w, norm2_w, w_in, dn_conv_w, dn_a_log, dn_dt_bias,
           dn_norm_w, na_qnorm_w, na_knorm_w, na_rpb, w_pa, w_pb, w_out, peer_wq, peer_keys, peer_u,
           peer_v):
    b, n_lat, d = x.shape
    n_ctx = ctx.shape[1]
    depth = ada_w.shape[0]
    s = n_ctx + n_lat
    assert n_ctx % TOK_TILE == 0 and n_lat % TOK_TILE == 0 and (b * s) % PEER_TB == 0
    assert n_lat % GRID_W == 0 and n_lat // GRID_W >= WIN_R and d == 1024

    stream = jnp.concatenate([ctx, x], axis=1)

    cc = jnp.zeros((16, d), F32).at[:b].set(c).at[b].set(c_ctx)
    mod = _modulation(cc, ada_w, ada_b).reshape(depth, 16, N_MOD, d)
    mod = jnp.pad(mod, ((0, 0), (0, 0), (0, 8 - N_MOD), (0, 0)))
    modsel = jnp.stack([jnp.broadcast_to(mod[:, b][:, None], (depth, b, 8, d)), mod[:, :b]], axis=2)

    cos_t, sin_t, e_a, e_b = _static_tables(n_ctx, n_lat)

    for l in range(depth):
        wl = w_in[l]
        o1 = 3 * DN_W + DN_W
        o2 = o1 + 4 * DN_HEADS
        w_in_p = jnp.concatenate(
            [wl[:, :o1], wl[:, o1:o2], jnp.zeros((d, AB_W - 4 * DN_HEADS), F32), wl[:, o2:]],
            axis=1).astype(BF16)
        dqkv, z, ab, nqkv, ga, gb = _in_projection(stream, modsel[l], norm1_w[l], w_in_p)

        qkvn = _dn_prep(dqkv, dn_conv_w[l], cos_t, sin_t, n_ctx)
        neg_a = jnp.repeat(-jnp.exp(dn_a_log[l]), DN_DK, axis=-1).reshape(2, 1, DN_W)
        dt_b = jnp.repeat(dn_dt_bias[l], DN_DK, axis=-1).reshape(2, 1, DN_W)
        o_dn = _dn_scan(qkvn, ab, e_a, e_b, neg_a, dt_b, n_ctx)

        o_na = _na_attention(nqkv, na_qnorm_w[l], na_knorm_w[l], _na_bias_table(na_rpb[l]), n_ctx)

        stream, h2, pq = _merge(stream, o_dn, z, o_na, ga, gb, modsel[l], dn_norm_w[l], norm2_w[l],
                                w_pa[l].astype(BF16), w_pb[l].astype(BF16), w_out[l].astype(BF16),
                                peer_wq[l].astype(BF16))

        g2_sub = jnp.broadcast_to(modsel[l][:, :, 5][:, jnp.minimum(jnp.arange(s // TOK_TILE), 1)],
                                  (b, s // TOK_TILE, d)).reshape(b * s // TOK_TILE, 1, d)
        keys_b = peer_keys[l].reshape(2 * PEER_HEADS, N_KEYS, PEER_DK).astype(BF16)
        out = _peer(stream.reshape(b * s, d), h2.reshape(b * s, d), pq.reshape(b * s, -1), g2_sub,
                    keys_b, peer_u[l].astype(BF16), peer_v[l].T.astype(BF16))
        stream = out.reshape(b, s, d)

    return stream[:, n_ctx:]
```

```python
import functools

import numpy as np
import jax
import jax.numpy as jnp
from jax import lax
from jax.experimental import pallas as pl
from jax.experimental.pallas import tpu as pltpu

F32 = jnp.float32
BF16 = jnp.bfloat16

GRID_W = 64
DN_HEADS = 8
DN_DK = 64
DN_W = DN_HEADS * DN_DK
DN_CHUNK = 64
CONV_K = 5
NA_HEADS = 8
NA_DH = 64
NA_W = NA_HEADS * NA_DH
WIN_R = 8
WIN_C = 16
ROPE_BASE = 10000.0
PEER_HEADS = 8
N_KEYS = 128
PEER_DK = 128
PEER_TOPK = 16
N_MOD = 6
EPS = 1e-6
NEG_INF = -1e30

LANES = 128
TOK_TILE = 256
DN_GROUP = 4
DN_GW = DN_GROUP * DN_DK
AB_W = 128
IN_W_PAD = 3 * DN_W + DN_W + AB_W + 3 * NA_W + 2 * 1024
NA_ROWS_PER_STEP = 4
PEER_TB = 512
PEER_EB = 1024
VMEM_LIMIT = 56 * 1024 * 1024


def _cp(sem, vmem=VMEM_LIMIT, flags=None):
    return pltpu.CompilerParams(dimension_semantics=sem, vmem_limit_bytes=vmem, flags=flags)


def _dot(a, b):
    return jnp.dot(a, b, preferred_element_type=F32)


def _dot_nt(a, b):
    return lax.dot_general(a, b, (((1,), (1,)), ((), ())), preferred_element_type=F32)


def _dot_tn(a, b):
    return lax.dot_general(a, b, (((0,), (0,)), ((), ())), preferred_element_type=F32)


def _split2(x):
    hi = x.astype(BF16)
    lo = (x - hi.astype(F32)).astype(BF16)
    return hi, lo


def _dot_exact_lhs(m_bf16, x):
    hi, lo = _split2(x)
    return _dot(m_bf16, hi) + _dot(m_bf16, lo)


def _dot_exact_rhs(x, m_bf16):
    hi, lo = _split2(x)
    return _dot(hi, m_bf16) + _dot(lo, m_bf16)


def _silu(x):
    return x * jax.nn.sigmoid(x)


def _softplus(x):
    return jnp.maximum(x, 0.0) + jnp.log1p(jnp.exp(-jnp.abs(x)))


def _log2(n):
    assert n & (n - 1) == 0
    return n.bit_length() - 1


def _same_group_ones(n, group):
    r = lax.broadcasted_iota(jnp.int32, (n, n), 0) >> _log2(group)
    c = lax.broadcasted_iota(jnp.int32, (n, n), 1) >> _log2(group)
    return jnp.where(r == c, 1.0, 0.0).astype(BF16)


def _mod_kernel(c_ref, w_ref, b_ref, o_ref):
    s = _silu(c_ref[...])
    o_ref[0] = _dot(s.astype(BF16), w_ref[0].astype(BF16)) + b_ref[0]


def _modulation(cc, ada_w, ada_b):
    depth, d, _ = ada_w.shape
    rows = cc.shape[0]
    return pl.pallas_call(
        _mod_kernel,
        out_shape=jax.ShapeDtypeStruct((depth, rows, N_MOD * d), F32),
        grid=(depth, N_MOD),
        in_specs=[
            pl.BlockSpec((rows, d), lambda l, j: (0, 0)),
            pl.BlockSpec((1, d, d), lambda l, j: (l, 0, j)),
            pl.BlockSpec((1, 1, d), lambda l, j: (l, 0, j)),
        ],
        out_specs=pl.BlockSpec((1, rows, d), lambda l, j: (l, 0, j)),
        compiler_params=_cp(("arbitrary", "arbitrary")),
        name="adaln_modulation",
    )(cc, ada_w, ada_b.reshape(depth, 1, N_MOD * d))


_IN_SPLITS = (3 * DN_W, DN_W, AB_W, 3 * NA_W, 1024, 1024)


def _inproj_kernel(x_ref, mod_ref, nw_ref, w_ref, *out_refs):
    x = x_ref[0]
    ms = jnp.mean(x * x, axis=-1, keepdims=True)
    y = x * lax.rsqrt(ms + EPS) * nw_ref[...]
    h = y * (1.0 + mod_ref[0, 0, 1:2, :]) + mod_ref[0, 0, 0:1, :]
    hb = h.astype(BF16)
    off = 0
    for o_ref, width in zip(out_refs, _IN_SPLITS):
        o_ref[0] = _dot(hb, w_ref[:, off:off + width])
        off += width


def _in_projection(stream, modsel, norm_w, w_in_p):
    b, s, d = stream.shape
    tm = TOK_TILE
    tok = lambda width: pl.BlockSpec((1, tm, width), lambda i, t: (i, t, 0))
    return pl.pallas_call(
        _inproj_kernel,
        out_shape=[jax.ShapeDtypeStruct((b, s, width), F32) for width in _IN_SPLITS],
        grid=(b, s // tm),
        in_specs=[
            tok(d),
            pl.BlockSpec((1, 1, 8, d), lambda i, t: (i, jnp.minimum(t, 1), 0, 0)),
            pl.BlockSpec((1, d), lambda i, t: (0, 0)),
            pl.BlockSpec((d, IN_W_PAD), lambda i, t: (0, 0)),
        ],
        out_specs=[tok(width) for width in _IN_SPLITS],
        compiler_params=_cp(("arbitrary", "arbitrary")),
        name="norm_in_projection",
    )(stream, modsel, norm_w.reshape(1, d), w_in_p)


def _dnprep_kernel(x_ref, cw_ref, cos_ref, sin_ref, o_ref, *, n_ctx):
    j = pl.program_id(1)
    x = x_ref[0]
    s = x.shape[0]
    t = lax.broadcasted_iota(jnp.int32, x.shape, 0)
    in_ctx = t < n_ctx
    pos = jnp.where(in_ctx, t, t - n_ctx)
    seg_len = jnp.where(in_ctx, n_ctx, s - n_ctx)
    y = jnp.zeros_like(x)
    for tap in range(CONV_K):
        d = tap - CONV_K // 2
        xs = x if d == 0 else pltpu.roll(x, (s - d) % s, 0)
        ok = (pos + d >= 0) & (pos + d < seg_len)
        y = y + jnp.where(ok, xs, 0.0) * cw_ref[tap:tap + 1, :]
    y = _silu(y)

    @pl.when(j < 2 * (DN_W // LANES))
    def _():
        ones = _same_group_ones(LANES, DN_DK)
        ss = _dot_exact_rhs(y * y, ones)
        yn = y * lax.rsqrt(ss + EPS)
        lane = lax.broadcasted_iota(jnp.int32, x.shape, 1)
        first_half = (lane & (DN_DK - 1)) < DN_DK // 2
        partner = jnp.where(first_half,
                            pltpu.roll(yn, LANES - DN_DK // 2, 1),
                            pltpu.roll(yn, DN_DK // 2, 1))
        out = yn * cos_ref[...] + partner * sin_ref[...]
        qscale = jnp.where(j < DN_W // LANES, DN_DK ** -0.5, 1.0)
        o_ref[0] = out * qscale

    @pl.when(j >= 2 * (DN_W // LANES))
    def _():
        o_ref[0] = y


def _dn_prep(dqkv, conv_w, cos_t, sin_t, n_ctx):
    b, s, w = dqkv.shape
    return pl.pallas_call(
        functools.partial(_dnprep_kernel, n_ctx=n_ctx),
        out_shape=jax.ShapeDtypeStruct((b, s, w), F32),
        grid=(b, w // LANES),
        in_specs=[
            pl.BlockSpec((1, s, LANES), lambda i, j: (i, 0, j)),
            pl.BlockSpec((CONV_K, LANES), lambda i, j: (0, j)),
            pl.BlockSpec((s, LANES), lambda i, j: (0, 0)),
            pl.BlockSpec((s, LANES), lambda i, j: (0, 0)),
        ],
        out_specs=pl.BlockSpec((1, s, LANES), lambda i, j: (i, 0, j)),
        compiler_params=_cp(("arbitrary", "arbitrary")),
        name="deltanet_prep",
    )(dqkv, conv_w, cos_t, sin_t)


def _tile_rows_bd(x, bd_mask):
    return jnp.where(bd_mask, jnp.concatenate([x] * DN_GROUP, axis=0), 0.0)


def _dnscan_kernel(qf_ref, kf_ref, vf_ref, abf_ref, qb_ref, kb_ref, vb_ref, abb_ref,
                   ea_ref, eb_ref, nega_ref, dtb_ref, of_ref, ob_ref, s_ref):
    c = DN_CHUNK
    gw = DN_GW
    n_groups = DN_HEADS // DN_GROUP

    @pl.when(pl.program_id(1) == 0)
    def _():
        s_ref[...] = jnp.zeros_like(s_ref)

    rows = lax.broadcasted_iota(jnp.int32, (c, gw), 0)
    cols = lax.broadcasted_iota(jnp.int32, (c, gw), 1) & (c - 1)
    rr = lax.broadcasted_iota(jnp.int32, (c, c), 0)
    cc = lax.broadcasted_iota(jnp.int32, (c, c), 1)
    ones8 = jnp.ones((8, c), BF16)
    bd_mask = (lax.broadcasted_iota(jnp.int32, (gw, gw), 0) >> _log2(c)) == (
        lax.broadcasted_iota(jnp.int32, (gw, gw), 1) >> _log2(c))
    bd = lambda x: _tile_rows_bd(x, bd_mask).astype(BF16)
    stack = lambda top, bottom: jnp.concatenate([top, bottom], axis=0).astype(BF16)

    dirs = []
    for direction, (q_ref, k_ref, v_ref, ab_ref, o_ref) in enumerate(
            ((qf_ref, kf_ref, vf_ref, abf_ref, of_ref), (qb_ref, kb_ref, vb_ref, abb_ref, ob_ref))):
        sign = 1 - 2 * direction
        lag = (rows - cols) * sign
        ab = ab_ref[0]
        a_exp = _dot_exact_rhs(ab, ea_ref[direction])
        b_exp = _dot_exact_rhs(ab, eb_ref[direction])
        dirs.append(dict(
            q_ref=q_ref, k_ref=k_ref, v_ref=v_ref, o_ref=o_ref, direction=direction,
            incl=lag >= 0, strict=lag > 0, eye=jnp.where(lag == 0, 1.0, 0.0),
            tri=jnp.where((rr - cc) * sign >= 0, 1.0, 0.0).astype(BF16),
            tri_t=jnp.where((cols - rows) * sign >= 0, 1.0, 0.0),
            g=nega_ref[direction] * _softplus(a_exp + dtb_ref[direction]),
            beta=jax.nn.sigmoid(b_exp)))

    chains = []
    for gi in range(n_groups):
        for dr in dirs:
            sl = slice(gi * gw, (gi + 1) * gw)
            chains.append(dict(dr, sl=sl, si=dr["direction"] * n_groups + gi,
                               q=dr["q_ref"][0, :, sl], k=dr["k_ref"][0, :, sl], v=dr["v_ref"][0, :, sl],
                               g=dr["g"][:, sl], beta=dr["beta"][:, sl]))

    for ch in chains:
        ch["g_row"] = _dot_exact_lhs(ch["tri"], ch["g"])
    for ch in chains:
        ch["g_col"] = _dot_exact_lhs(ones8, ch["g"] * ch["tri_t"])[0:1]
        ch["g_tot"] = _dot_exact_lhs(ones8, ch["g"])[0:1]
    for ch in chains:
        ch["kb"] = ch["k"] * ch["beta"]
        ch["kq"] = _dot_nt(stack(ch["kb"], ch["q"]), bd(ch["k"]))
    for ch in chains:
        incl = ch["incl"]
        decay = jnp.where(incl, jnp.exp(jnp.where(incl, ch["g_row"] - ch["g_col"], 0.0)), 0.0)
        ch["a_mat"] = jnp.where(ch["strict"], ch["kq"][:c] * decay, 0.0)
        ch["intra"] = jnp.where(incl, ch["kq"][c:] * decay, 0.0)
        ch["t_mat"] = ch["eye"] - ch["a_mat"]
    for ch in chains:
        ch["x"] = _dot(ch["a_mat"].astype(BF16), bd(ch["a_mat"]))
    for _ in range(4):
        for ch in chains:
            r = _dot(stack(ch["x"], ch["t_mat"]), bd(ch["x"]))
            ch["x"] = r[:c]
            ch["t_mat"] = ch["t_mat"] + r[c:]
    for ch in chains:
        ch["t_mat"] = (ch["t_mat"] + _dot(ch["t_mat"].astype(BF16), bd(ch["x"]))).astype(BF16)
    for ch in chains:
        ch["e_row"] = jnp.exp(ch["g_row"])
        ch["w_val"] = _dot(ch["t_mat"], bd(ch["v"] * ch["beta"]))
        ch["u_key"] = _dot(ch["t_mat"], bd(ch["kb"] * ch["e_row"]))
    for ch in chains:
        ch["state"] = s_ref[ch["si"]]
        ch["r2"] = _dot(stack(ch["u_key"], ch["q"] * ch["e_row"]), ch["state"].astype(BF16))
    for ch in chains:
        ch["v_new"] = ch["w_val"] - ch["r2"][:c]
        ch["o"] = ch["r2"][c:] + _dot(ch["intra"].astype(BF16), bd(ch["v_new"]))
        k_dec = ch["k"] * jnp.exp(ch["g_tot"] - ch["g_row"])
        ch["upd"] = _dot_tn(k_dec.astype(BF16), ch["v_new"].astype(BF16))
    for ch in chains:
        s_ref[ch["si"]] = ch["state"] * jnp.exp(ch["g_tot"]) + jnp.where(bd_mask, ch["upd"], 0.0)
        ch["o_ref"][0, :, ch["sl"]] = ch["o"]


def _dn_scan(qkvn, ab, e_a, e_b, neg_a, dt_b, n_ctx):
    b, s, _ = qkvn.shape
    c = DN_CHUNK
    n_chunks = s // c
    ctx_chunks = n_ctx // c

    def fwd(t):
        return t

    def bwd(t):
        return jnp.where(t < ctx_chunks, ctx_chunks - 1 - t, n_chunks - 1 + ctx_chunks - t)

    def tok(order, part, width):
        return pl.BlockSpec((1, c, width), lambda i, t: (i, order(t), part))

    full = lambda arr: pl.BlockSpec(arr.shape, lambda i, t: (0,) * arr.ndim)
    out = jax.ShapeDtypeStruct((b, s, DN_W), F32)
    return pl.pallas_call(
        _dnscan_kernel,
        out_shape=[out, out],
        grid=(b, n_chunks),
        in_specs=[
            tok(fwd, 0, DN_W), tok(fwd, 1, DN_W), tok(fwd, 2, DN_W), tok(fwd, 0, AB_W),
            tok(bwd, 0, DN_W), tok(bwd, 1, DN_W), tok(bwd, 2, DN_W), tok(bwd, 0, AB_W),
            full(e_a), full(e_b), full(neg_a), full(dt_b),
        ],
        out_specs=[tok(fwd, 0, DN_W), tok(bwd, 0, DN_W)],
        scratch_shapes=[pltpu.VMEM((2 * (DN_HEADS // DN_GROUP), DN_GW, DN_GW), F32)],
        compiler_params=_cp(("arbitrary", "arbitrary")),
        name="deltanet_scan",
    )(qkvn, qkvn, qkvn, ab, qkvn, qkvn, qkvn, ab, e_a, e_b, neg_a, dt_b)


def _na_kernel(q_ref, k_ref, v_ref, qw_ref, kw_ref, bias_ref, o_ref, qm_ref, kn_ref, vm_ref,
               *, n_ctx, rows):
    lane = lax.broadcasted_iota(jnp.int32, (1, LANES), 1)
    head0 = lane < NA_DH
    ones = _same_group_ones(LANES, NA_DH)

    def head_norm(x, w):
        ms = _dot_exact_rhs(x * x, ones) * (1.0 / NA_DH)
        return x * lax.rsqrt(ms + EPS) * w

    q = head_norm(q_ref[0], qw_ref[...]) * NA_DH ** -0.5
    qm_ref[0] = jnp.where(head0, q, 0.0).astype(BF16)
    qm_ref[1] = jnp.where(head0, 0.0, q).astype(BF16)
    kn_ref[...] = head_norm(k_ref[0], kw_ref[...]).astype(BF16)
    v = v_ref[0]
    vm_ref[0] = jnp.where(head0, v, 0.0).astype(BF16)
    vm_ref[1] = jnp.where(head0, 0.0, v).astype(BF16)

    k_ctx = kn_ref[0:n_ctx, :]
    wr = WIN_R
    band = wr * GRID_W

    def attend(chains):
        for ch in chains:
            ch["s"] = [_dot_nt(ch["q"], k) if bias is None else _dot_nt(ch["q"], k) + bias
                       for k, _, bias in ch["keys"]]
        for ch in chains:
            m = functools.reduce(jnp.maximum, [jnp.max(sc, axis=-1, keepdims=True) for sc in ch["s"]])
            ch["p"] = [jnp.exp(sc - m) for sc in ch["s"]]
            ch["l"] = functools.reduce(jnp.add, [jnp.sum(p, axis=-1, keepdims=True) for p in ch["p"]])
        for ch in chains:
            ch["o"] = functools.reduce(jnp.add, [_dot(p.astype(BF16), v)
                                                 for p, (_, v, _) in zip(ch["p"], ch["keys"])])
        return [ch["o"] / ch["l"] for ch in chains]

    v_ctx = [vm_ref[h, 0:n_ctx, :] for h in range(2)]

    n_blk = n_ctx // GRID_W
    outs = attend([dict(q=qm_ref[h, blk * GRID_W:(blk + 1) * GRID_W, :], keys=[(k_ctx, v_ctx[h], None)])
                   for blk in range(n_blk) for h in range(2)])
    for blk in range(n_blk):
        o_ref[0, blk * GRID_W:(blk + 1) * GRID_W, :] = outs[2 * blk] + outs[2 * blk + 1]

    def rows_body(it, carry):
        chains, q0s = [], []
        for j in range(NA_ROWS_PER_STEP):
            r = it * NA_ROWS_PER_STEP + j
            rs = jnp.clip(r - wr // 2, 0, rows - wr)
            var = rs - r + (WIN_R - 1)
            q0 = pl.multiple_of(n_ctx + r * GRID_W, GRID_W)
            k0 = pl.multiple_of(n_ctx + rs * GRID_W, GRID_W)
            k_band = kn_ref[pl.ds(k0, band), :]
            q0s.append(q0)
            for h in range(2):
                chains.append(dict(q=qm_ref[h, pl.ds(q0, GRID_W), :],
                                   keys=[(k_band, vm_ref[h, pl.ds(k0, band), :], bias_ref[h, var]),
                                         (k_ctx, v_ctx[h], None)]))
        outs = attend(chains)
        for j, q0 in enumerate(q0s):
            o_ref[0, pl.ds(q0, GRID_W), :] = outs[2 * j] + outs[2 * j + 1]
        return carry

    lax.fori_loop(0, rows // NA_ROWS_PER_STEP, rows_body, 0)


def _na_attention(nqkv, q_w, k_w, bias_tab, n_ctx):
    b, s, _ = nqkv.shape
    rows = (s - n_ctx) // GRID_W
    pairs = NA_W // LANES
    tok = lambda part: pl.BlockSpec((1, s, LANES), lambda i, p: (i, 0, part * pairs + p))
    return pl.pallas_call(
        functools.partial(_na_kernel, n_ctx=n_ctx, rows=rows),
        out_shape=jax.ShapeDtypeStruct((b, s, NA_W), F32),
        grid=(b, pairs),
        in_specs=[
            tok(0), tok(1), tok(2),
            pl.BlockSpec((1, LANES), lambda i, p: (0, 0)),
            pl.BlockSpec((1, LANES), lambda i, p: (0, 0)),
            pl.BlockSpec((2, WIN_R, GRID_W, WIN_R * GRID_W), lambda i, p: (p, 0, 0, 0)),
        ],
        out_specs=pl.BlockSpec((1, s, LANES), lambda i, p: (i, 0, p)),
        scratch_shapes=[pltpu.VMEM((2, s, LANES), BF16), pltpu.VMEM((s, LANES), BF16),
                        pltpu.VMEM((2, s, LANES), BF16)],
        compiler_params=_cp(("arbitrary", "arbitrary")),
        name="neighbourhood_attention",
    )(nqkv, nqkv, nqkv, jnp.tile(q_w, 2).reshape(1, LANES), jnp.tile(k_w, 2).reshape(1, LANES),
      bias_tab)


def _merge_kernel(x_ref, odf_ref, odb_ref, z_ref, ona_ref, ga_ref, gb_ref, mod_ref, dnw_ref, n2w_ref,
                  ones_ref, wpa_ref, wpb_ref, wout_ref, wq_ref, xo_ref, h2_ref, pq_ref):
    o_dn = odf_ref[0] + odb_ref[0]
    parts = []
    for j in range(DN_W // LANES):
        oj = o_dn[:, j * LANES:(j + 1) * LANES]
        ms = _dot_exact_rhs(oj * oj, ones_ref[...]) * (1.0 / DN_DK)
        parts.append(oj * lax.rsqrt(ms + EPS))
    o_a = jnp.concatenate(parts, axis=1) * dnw_ref[...] * _silu(z_ref[0])
    y = (jax.nn.sigmoid(ga_ref[0]) * _dot(o_a.astype(BF16), wpa_ref[...])
         + jax.nn.sigmoid(gb_ref[0]) * _dot(ona_ref[0].astype(BF16), wpb_ref[...]))
    x = x_ref[0] + mod_ref[0, 0, 2:3, :] * _dot(y.astype(BF16), wout_ref[...])
    xo_ref[0] = x
    ms = jnp.mean(x * x, axis=-1, keepdims=True)
    h2 = (x * lax.rsqrt(ms + EPS) * n2w_ref[...]) * (1.0 + mod_ref[0, 0, 4:5, :]) + mod_ref[0, 0, 3:4, :]
    h2b = h2.astype(BF16)
    h2_ref[0] = h2b
    pq_ref[0] = _dot(h2b, wq_ref[...]).astype(BF16)


def _merge(stream, o_dn_f, o_dn_b, z, o_na, ga, gb, modsel, dn_norm_w, norm2_w, w_pa, w_pb, w_out, w_q):
    b, s, d = stream.shape
    tm = TOK_TILE
    qw = w_q.shape[1]
    tok = lambda width: pl.BlockSpec((1, tm, width), lambda i, t: (i, t, 0))
    full = lambda arr: pl.BlockSpec(arr.shape, lambda i, t: (0,) * arr.ndim)
    dnw = jnp.tile(dn_norm_w, DN_HEADS).reshape(1, DN_W)
    n2w = norm2_w.reshape(1, d)
    ones = jnp.asarray(np.kron(np.eye(LANES // DN_DK), np.ones((DN_DK, DN_DK))), BF16)
    return pl.pallas_call(
        _merge_kernel,
        out_shape=[jax.ShapeDtypeStruct((b, s, d), F32), jax.ShapeDtypeStruct((b, s, d), BF16),
                   jax.ShapeDtypeStruct((b, s, qw), BF16)],
        grid=(b, s // tm),
        in_specs=[
            tok(d), tok(DN_W), tok(DN_W),
            tok(DN_W), tok(NA_W), tok(d), tok(d),
            pl.BlockSpec((1, 1, 8, d), lambda i, t: (i, jnp.minimum(t, 1), 0, 0)),
            full(dnw), full(n2w), full(ones), full(w_pa), full(w_pb), full(w_out), full(w_q),
        ],
        out_specs=[tok(d), tok(d), tok(qw)],
        compiler_params=_cp(("arbitrary", "arbitrary")),
        name="merge_projection",
    )(stream, o_dn_f, o_dn_b, z, o_na, ga, gb, modsel, dnw, n2w, ones, w_pa, w_pb, w_out, w_q)


_LOG2E = 1.4426950408889634
_GELU_K1 = -2.0 * 0.7978845608028654 * _LOG2E
_GELU_K3 = _GELU_K1 * 0.044715


def _gelu_tanh(x):
    return x / (1.0 + jnp.exp2(x * (_GELU_K1 + _GELU_K3 * (x * x))))


def _peer_candidates(t1, t2, fill):
    sub = lax.broadcasted_iota(jnp.int32, (8, LANES), 0)
    cands = [t1[0:1] + t2[0:8], t1[0:1] + t2[8:16], t1[1:2] + t2[0:8]]
    for i in range(2, 8):
        cands.append(jnp.where(sub < PEER_TOPK // (i + 1), t1[i:i + 1] + t2[0:8], fill))
    cands.append(t1[8:16] + t2[0:1])
    return cands


def _peer_select_chunk(s1, s2, t1_ref, t2_ref):
    neg = jnp.float32(-jnp.inf)
    for t_ref, s in ((t1_ref, s1), (t2_ref, s2)):
        cur = s
        for kk in range(PEER_TOPK + 1):
            m = jnp.max(cur, axis=0, keepdims=True)
            t_ref[kk:kk + 1, :] = m
            if kk < PEER_TOPK:
                cur = jnp.where(cur == m, neg, cur)
    t1 = t1_ref[0:PEER_TOPK, :]
    t2 = t2_ref[0:PEER_TOPK, :]
    t1_next = t1_ref[PEER_TOPK:PEER_TOPK + 1, :]
    t2_next = t2_ref[PEER_TOPK:PEER_TOPK + 1, :]
    cands = _peer_candidates(t1, t2, neg)
    cur = cands
    tau = None
    for kk in range(PEER_TOPK):
        m = jnp.max(functools.reduce(jnp.maximum, cur), axis=0, keepdims=True)
        if kk + 1 < PEER_TOPK:
            cur = [jnp.where(cd == m, neg, cd) for cd in cur]
        else:
            tau = m
    m1 = t1[0:1]
    m2 = t2[0:1]
    top = m1 + m2
    z = functools.reduce(jnp.add, [
        jnp.sum(jnp.where(cd >= tau, jnp.exp(cd - top), 0.0), axis=0, keepdims=True) for cd in cands])
    log2_z = jnp.log(z) * _LOG2E
    to_p1 = lambda s: (s - m1) * _LOG2E - log2_z
    to_p2 = lambda s: (s - m2) * _LOG2E
    cands_p = _peer_candidates(to_p1(t1), to_p2(t2), neg)
    low = functools.reduce(jnp.minimum, [
        jnp.min(jnp.where(cd >= tau, cp, jnp.inf), axis=0, keepdims=True) for cd, cp in zip(cands, cands_p)])
    below = functools.reduce(jnp.maximum, [
        jnp.max(jnp.where(cd >= tau, neg, cp), axis=0, keepdims=True) for cd, cp in zip(cands, cands_p)])
    below = jnp.maximum(below, jnp.maximum(to_p1(t1[0:1]) + to_p2(t2_next), to_p1(t1_next) + to_p2(t2[0:1])))
    return to_p1(s1), to_p2(s2), 0.5 * (low + below)


def _peer_kernel(x_ref, h2_ref, pq_ref, g2_ref, keys_ref, u_ref, vt_ref, o_ref,
                 s_ref, th_ref, e1_ref, p2_ref, e2_ref, t1_ref, t2_ref, act_ref, g_ref, acc_ref, *, sub_tile):
    e = pl.program_id(1)
    tb = h2_ref.shape[0]
    n_chunk = tb // LANES
    eb = u_ref.shape[0]

    @pl.when(e == 0)
    def _():
        acc_ref[...] = jnp.zeros_like(acc_ref)
        for hp in range(2 * PEER_HEADS):
            st = _dot_nt(keys_ref[hp], pq_ref[:, hp * PEER_DK:(hp + 1) * PEER_DK])
            for ci in range(n_chunk):
                s_ref[hp, ci] = st[:, ci * LANES:(ci + 1) * LANES]

        def select_body(h, carry):
            for ci in range(n_chunk):
                p1, p2, thr = _peer_select_chunk(s_ref[2 * h, ci], s_ref[2 * h + 1, ci],
                                                 t1_ref.at[ci], t2_ref.at[ci])
                th_ref[h, ci] = thr - p1
                e1_ref[h, ci] = jnp.exp2(p1)
                p2_ref[h, ci] = p2
                e2_ref[h, ci] = jnp.exp2(p2)
            return carry

        lax.fori_loop(0, PEER_HEADS, select_body, 0)

    act_ref[...] = _gelu_tanh(_dot_nt(u_ref[...], h2_ref[...]))

    def gate_body(al, carry):
        a = e * (eb // N_KEYS) + al
        r0 = pl.multiple_of(al * N_KEYS, N_KEYS)
        for ci in range(n_chunk):
            cols = slice(ci * LANES, (ci + 1) * LANES)
            w = jnp.zeros((N_KEYS, LANES), F32)
            for h in range(PEER_HEADS):
                picked = p2_ref[h, ci] >= th_ref[h, ci, pl.ds(a, 1), :]
                w = w + jnp.where(picked, e2_ref[h, ci] * e1_ref[h, ci, pl.ds(a, 1), :], 0.0)
            g_ref[pl.ds(r0, N_KEYS), cols] = (w * act_ref[pl.ds(r0, N_KEYS), cols]).astype(BF16)
        return carry

    lax.fori_loop(0, eb // N_KEYS, gate_body, 0)
    acc_ref[...] += _dot(vt_ref[...], g_ref[...])

    @pl.when(e == pl.num_programs(1) - 1)
    def _():
        f = acc_ref[...].T
        for si in range(tb // sub_tile):
            rows = slice(si * sub_tile, (si + 1) * sub_tile)
            o_ref[rows, :] = x_ref[rows, :] + g2_ref[si] * f[rows, :]


def _peer(x_flat, h2_flat, pq_flat, g2_sub, keys_b, u_b, v_b):
    t, d = x_flat.shape
    tb = PEER_TB
    eb = PEER_EB
    n_exp = u_b.shape[0]
    vt_b = v_b.T
    n_chunk = tb // LANES
    n_sub = tb // TOK_TILE
    return pl.pallas_call(
        functools.partial(_peer_kernel, sub_tile=TOK_TILE),
        out_shape=jax.ShapeDtypeStruct((t, d), F32),
        grid=(t // tb, n_exp // eb),
        in_specs=[
            pl.BlockSpec((tb, d), lambda i, e: (i, 0)),
            pl.BlockSpec((tb, d), lambda i, e: (i, 0)),
            pl.BlockSpec((tb, pq_flat.shape[1]), lambda i, e: (i, 0)),
            pl.BlockSpec((n_sub, 1, d), lambda i, e: (i, 0, 0)),
            pl.BlockSpec(keys_b.shape, lambda i, e: (0, 0, 0)),
            pl.BlockSpec((eb, d), lambda i, e: (e, 0)),
            pl.BlockSpec((d, eb), lambda i, e: (0, e)),
        ],
        out_specs=pl.BlockSpec((tb, d), lambda i, e: (i, 0)),
        scratch_shapes=[
            pltpu.VMEM((2 * PEER_HEADS, n_chunk, N_KEYS, LANES), F32),
            pltpu.VMEM((PEER_HEADS, n_chunk, N_KEYS, LANES), F32),
            pltpu.VMEM((PEER_HEADS, n_chunk, N_KEYS, LANES), F32),
            pltpu.VMEM((PEER_HEADS, n_chunk, N_KEYS, LANES), F32),
            pltpu.VMEM((PEER_HEADS, n_chunk, N_KEYS, LANES), F32),
            pltpu.VMEM((n_chunk, PEER_TOPK + 8, LANES), F32),
            pltpu.VMEM((n_chunk, PEER_TOPK + 8, LANES), F32),
            pltpu.VMEM((eb, tb), F32),
            pltpu.VMEM((eb, tb), BF16),
            pltpu.VMEM((d, tb), F32),
        ],
        compiler_params=_cp(("arbitrary", "arbitrary")),
        name="peer_dense",
    )(x_flat, h2_flat, pq_flat, g2_sub, keys_b, u_b, vt_b)


def _static_tables(n_ctx, n_lat):
    n_freq = DN_DK // 4
    t = np.arange(n_lat)
    row = (t // GRID_W).astype(np.float32)
    col = (t % GRID_W).astype(np.float32)
    inv_freq = (np.float32(ROPE_BASE) ** (-np.arange(n_freq, dtype=np.float32) / n_freq)).astype(np.float32)
    ang = np.concatenate([row[:, None] * inv_freq, col[:, None] * inv_freq], axis=-1)
    cos_h = np.concatenate([np.cos(ang), np.cos(ang)], axis=-1)
    sin_h = np.concatenate([-np.sin(ang), np.sin(ang)], axis=-1)
    cos_t = np.concatenate([np.ones((n_ctx, DN_DK), np.float32), cos_h], axis=0)
    sin_t = np.concatenate([np.zeros((n_ctx, DN_DK), np.float32), sin_h], axis=0)
    cos_t = jnp.asarray(np.tile(cos_t, (1, LANES // DN_DK)), F32)
    sin_t = jnp.asarray(np.tile(sin_t, (1, LANES // DN_DK)), F32)
    sel = np.zeros((2, 2, AB_W, DN_W), np.float32)
    for which in range(2):
        for dr in range(2):
            for h in range(DN_HEADS):
                sel[which, dr, which * 2 * DN_HEADS + dr * DN_HEADS + h, h * DN_DK:(h + 1) * DN_DK] = 1.0
    return cos_t, sin_t, jnp.asarray(sel[0], BF16), jnp.asarray(sel[1], BF16)


def _na_bias_table(rpb):
    c = np.arange(GRID_W)
    kc = np.arange(GRID_W)
    cs = np.clip(c - WIN_C // 2, 0, GRID_W - WIN_C)
    in_win = (kc[None, :] >= cs[:, None]) & (kc[None, :] < cs[:, None] + WIN_C)
    dc = np.clip(kc[None, :] - c[:, None] + (WIN_C - 1), 0, 2 * WIN_C - 2)
    var = np.arange(WIN_R)
    jj = np.arange(WIN_R)
    dr = var[:, None] + jj[None, :]
    tab = rpb[:, dr][:, :, :, dc]
    tab = jnp.where(in_win[None, None, None], tab, NEG_INF)
    tab = tab.transpose(0, 1, 3, 2, 4)
    return tab.reshape(rpb.shape[0], WIN_R, GRID_W, WIN_R * GRID_W)


def kernel(x, c, ctx, c_ctx, ada_w, ada_b, norm1_w, norm2_w, w_in, dn_conv_w, dn_a_log, dn_dt_bias,
           dn_norm_w, na_qnorm_w, na_knorm_w, na_rpb, w_pa, w_pb, w_out, peer_wq, peer_keys, peer_u,
           peer_v):
    b, n_lat, d = x.shape
    n_ctx = ctx.shape[1]
    depth = ada_w.shape[0]
    s = n_ctx + n_lat
    assert n_ctx % TOK_TILE == 0 and n_lat % TOK_TILE == 0 and (b * s) % PEER_TB == 0
    assert n_lat % GRID_W == 0 and n_lat // GRID_W >= WIN_R and d == 1024

    stream = jnp.concatenate([ctx, x], axis=1)

    cc = jnp.zeros((16, d), F32).at[:b].set(c).at[b].set(c_ctx)
    mod = _modulation(cc, ada_w, ada_b).reshape(depth, 16, N_MOD, d)
    mod = jnp.pad(mod, ((0, 0), (0, 0), (0, 8 - N_MOD), (0, 0)))
    modsel = jnp.stack([jnp.broadcast_to(mod[:, b][:, None], (depth, b, 8, d)), mod[:, :b]], axis=2)

    cos_t, sin_t, e_a, e_b = _static_tables(n_ctx, n_lat)

    for l in range(depth):
        wl = w_in[l]
        o1 = 3 * DN_W + DN_W
        o2 = o1 + 4 * DN_HEADS
        w_in_p = jnp.concatenate(
            [wl[:, :o1], wl[:, o1:o2], jnp.zeros((d, AB_W - 4 * DN_HEADS), F32), wl[:, o2:]],
            axis=1).astype(BF16)
        dqkv, z, ab, nqkv, ga, gb = _in_projection(stream, modsel[l], norm1_w[l], w_in_p)

        qkvn = _dn_prep(dqkv, dn_conv_w[l], cos_t, sin_t, n_ctx)
        neg_a = jnp.repeat(-jnp.exp(dn_a_log[l]), DN_DK, axis=-1).reshape(2, 1, DN_W)
        dt_b = jnp.repeat(dn_dt_bias[l], DN_DK, axis=-1).reshape(2, 1, DN_W)
        o_dn_f, o_dn_b = _dn_scan(qkvn, ab, e_a, e_b, neg_a, dt_b, n_ctx)

        o_na = _na_attention(nqkv, na_qnorm_w[l], na_knorm_w[l], _na_bias_table(na_rpb[l]), n_ctx)

        stream, h2, pq = _merge(stream, o_dn_f, o_dn_b, z, o_na, ga, gb, modsel[l], dn_norm_w[l], norm2_w[l],
                                w_pa[l].astype(BF16), w_pb[l].astype(BF16), w_out[l].astype(BF16),
                                peer_wq[l].astype(BF16))

        g2_sub = jnp.broadcast_to(modsel[l][:, :, 5][:, jnp.minimum(jnp.arange(s // TOK_TILE), 1)],
                                  (b, s // TOK_TILE, d)).reshape(b * s // TOK_TILE, 1, d)
        keys_b = peer_keys[l].reshape(2 * PEER_HEADS, N_KEYS, PEER_DK).astype(BF16)
        out = _peer(stream.reshape(b * s, d), h2.reshape(b * s, d), pq.reshape(b * s, -1), g2_sub,
                    keys_b, peer_u[l].astype(BF16), peer_v[l].astype(BF16))
        stream = out.reshape(b, s, d)

    return stream[:, n_ctx:]
```

```python
import functools

import numpy as np
import jax
import jax.numpy as jnp
from jax import lax
from jax.experimental import pallas as pl
from jax.experimental.pallas import tpu as pltpu

F32 = jnp.float32
BF16 = jnp.bfloat16

GRID_W = 64
DN_HEADS = 8
DN_DK = 64
DN_W = DN_HEADS * DN_DK
DN_CHUNK = 64
CONV_K = 5
NA_HEADS = 8
NA_DH = 64
NA_W = NA_HEADS * NA_DH
WIN_R = 8
WIN_C = 16
ROPE_BASE = 10000.0
PEER_HEADS = 8
N_KEYS = 128
PEER_DK = 128
PEER_TOPK = 16
N_MOD = 6
EPS = 1e-6
NEG_INF = -1e30

LANES = 128
TOK_TILE = 256
DN_GROUP = 4
DN_GW = DN_GROUP * DN_DK
DN_CHUNKS_PER_STEP = 4
AB_W = 128
IN_W_PAD = 3 * DN_W + DN_W + AB_W + 3 * NA_W + 2 * 1024
NA_ROWS_PER_STEP = 8
PEER_TB = 512
PEER_EB = 1024
VMEM_LIMIT = 56 * 1024 * 1024


def _cp(sem, vmem=VMEM_LIMIT, flags=None):
    return pltpu.CompilerParams(dimension_semantics=sem, vmem_limit_bytes=vmem, flags=flags)


def _dot(a, b):
    return jnp.dot(a, b, preferred_element_type=F32)


def _dot_nt(a, b):
    return lax.dot_general(a, b, (((1,), (1,)), ((), ())), preferred_element_type=F32)


def _dot_tn(a, b):
    return lax.dot_general(a, b, (((0,), (0,)), ((), ())), preferred_element_type=F32)


def _split2(x):
    hi = x.astype(BF16)
    lo = (x - hi.astype(F32)).astype(BF16)
    return hi, lo


def _dot_exact_lhs(m_bf16, x):
    hi, lo = _split2(x)
    return _dot(m_bf16, hi) + _dot(m_bf16, lo)


def _dot_exact_rhs(x, m_bf16):
    hi, lo = _split2(x)
    return _dot(hi, m_bf16) + _dot(lo, m_bf16)


def _silu(x):
    return x * jax.nn.sigmoid(x)


def _softplus(x):
    return jnp.maximum(x, 0.0) + jnp.log1p(jnp.exp(-jnp.abs(x)))


def _log2(n):
    assert n & (n - 1) == 0
    return n.bit_length() - 1


def _same_group_ones(n, group):
    r = lax.broadcasted_iota(jnp.int32, (n, n), 0) >> _log2(group)
    c = lax.broadcasted_iota(jnp.int32, (n, n), 1) >> _log2(group)
    return jnp.where(r == c, 1.0, 0.0).astype(BF16)


def _mod_kernel(c_ref, w_ref, b_ref, o_ref):
    s = _silu(c_ref[...])
    o_ref[0] = _dot(s.astype(BF16), w_ref[0].astype(BF16)) + b_ref[0]


def _modulation(cc, ada_w, ada_b):
    depth, d, _ = ada_w.shape
    rows = cc.shape[0]
    return pl.pallas_call(
        _mod_kernel,
        out_shape=jax.ShapeDtypeStruct((depth, rows, N_MOD * d), F32),
        grid=(depth, N_MOD),
        in_specs=[
            pl.BlockSpec((rows, d), lambda l, j: (0, 0)),
            pl.BlockSpec((1, d, d), lambda l, j: (l, 0, j)),
            pl.BlockSpec((1, 1, d), lambda l, j: (l, 0, j)),
        ],
        out_specs=pl.BlockSpec((1, rows, d), lambda l, j: (l, 0, j)),
        compiler_params=_cp(("arbitrary", "arbitrary")),
        name="adaln_modulation",
    )(cc, ada_w, ada_b.reshape(depth, 1, N_MOD * d))


_IN_SPLITS = (3 * DN_W, DN_W, AB_W, 3 * NA_W, 1024, 1024)


def _inproj_kernel(x_ref, mod_ref, nw_ref, w_ref, *out_refs):
    x = x_ref[0]
    ms = jnp.mean(x * x, axis=-1, keepdims=True)
    y = x * lax.rsqrt(ms + EPS) * nw_ref[...]
    h = y * (1.0 + mod_ref[0, 0, 1:2, :]) + mod_ref[0, 0, 0:1, :]
    hb = h.astype(BF16)
    off = 0
    for o_ref, width in zip(out_refs, _IN_SPLITS):
        o_ref[0] = _dot(hb, w_ref[:, off:off + width])
        off += width


def _in_projection(stream, modsel, norm_w, w_in_p):
    b, s, d = stream.shape
    tm = TOK_TILE
    tok = lambda width: pl.BlockSpec((1, tm, width), lambda i, t: (i, t, 0))
    return pl.pallas_call(
        _inproj_kernel,
        out_shape=[jax.ShapeDtypeStruct((b, s, width), F32) for width in _IN_SPLITS],
        grid=(b, s // tm),
        in_specs=[
            tok(d),
            pl.BlockSpec((1, 1, 8, d), lambda i, t: (i, jnp.minimum(t, 1), 0, 0)),
            pl.BlockSpec((1, d), lambda i, t: (0, 0)),
            pl.BlockSpec((d, IN_W_PAD), lambda i, t: (0, 0)),
        ],
        out_specs=[tok(width) for width in _IN_SPLITS],
        compiler_params=_cp(("arbitrary", "arbitrary")),
        name="norm_in_projection",
    )(stream, modsel, norm_w.reshape(1, d), w_in_p)


def _dnprep_kernel(x_ref, cw_ref, cos_ref, sin_ref, o_ref, *, n_ctx):
    j = pl.program_id(1)
    x = x_ref[0]
    s = x.shape[0]
    t = lax.broadcasted_iota(jnp.int32, x.shape, 0)
    in_ctx = t < n_ctx
    pos = jnp.where(in_ctx, t, t - n_ctx)
    seg_len = jnp.where(in_ctx, n_ctx, s - n_ctx)
    y = jnp.zeros_like(x)
    for tap in range(CONV_K):
        d = tap - CONV_K // 2
        xs = x if d == 0 else pltpu.roll(x, (s - d) % s, 0)
        ok = (pos + d >= 0) & (pos + d < seg_len)
        y = y + jnp.where(ok, xs, 0.0) * cw_ref[tap:tap + 1, :]
    y = _silu(y)

    @pl.when(j < 2 * (DN_W // LANES))
    def _():
        ones = _same_group_ones(LANES, DN_DK)
        ss = _dot_exact_rhs(y * y, ones)
        yn = y * lax.rsqrt(ss + EPS)
        lane = lax.broadcasted_iota(jnp.int32, x.shape, 1)
        first_half = (lane & (DN_DK - 1)) < DN_DK // 2
        partner = jnp.where(first_half,
                            pltpu.roll(yn, LANES - DN_DK // 2, 1),
                            pltpu.roll(yn, DN_DK // 2, 1))
        out = yn * cos_ref[...] + partner * sin_ref[...]
        qscale = jnp.where(j < DN_W // LANES, DN_DK ** -0.5, 1.0)
        o_ref[0] = out * qscale

    @pl.when(j >= 2 * (DN_W // LANES))
    def _():
        o_ref[0] = y


def _dn_prep(dqkv, conv_w, cos_t, sin_t, n_ctx):
    b, s, w = dqkv.shape
    return pl.pallas_call(
        functools.partial(_dnprep_kernel, n_ctx=n_ctx),
        out_shape=jax.ShapeDtypeStruct((b, s, w), F32),
        grid=(b, w // LANES),
        in_specs=[
            pl.BlockSpec((1, s, LANES), lambda i, j: (i, 0, j)),
            pl.BlockSpec((CONV_K, LANES), lambda i, j: (0, j)),
            pl.BlockSpec((s, LANES), lambda i, j: (0, 0)),
            pl.BlockSpec((s, LANES), lambda i, j: (0, 0)),
        ],
        out_specs=pl.BlockSpec((1, s, LANES), lambda i, j: (i, 0, j)),
        compiler_params=_cp(("arbitrary", "arbitrary")),
        name="deltanet_prep",
    )(dqkv, conv_w, cos_t, sin_t)


def _tile_rows_bd(x, bd_mask):
    return jnp.where(bd_mask, jnp.concatenate([x] * DN_GROUP, axis=0), 0.0)


def _dnscan_kernel(qf_ref, kf_ref, vf_ref, abf_ref, qb_ref, kb_ref, vb_ref, abb_ref,
                   e_ref, nega_ref, dtb_ref, of_ref, ob_ref, s_ref):
    c = DN_CHUNK
    gw = DN_GW
    n_groups = DN_HEADS // DN_GROUP

    @pl.when(pl.program_id(1) == 0)
    def _():
        s_ref[...] = jnp.zeros_like(s_ref)

    rows = lax.broadcasted_iota(jnp.int32, (c, gw), 0)
    cols = lax.broadcasted_iota(jnp.int32, (c, gw), 1) & (c - 1)
    rr = lax.broadcasted_iota(jnp.int32, (c, c), 0)
    cc = lax.broadcasted_iota(jnp.int32, (c, c), 1)
    ones8 = jnp.ones((8, c), BF16)
    bd_mask = (lax.broadcasted_iota(jnp.int32, (gw, gw), 0) >> _log2(c)) == (
        lax.broadcasted_iota(jnp.int32, (gw, gw), 1) >> _log2(c))
    bd = lambda x: _tile_rows_bd(x, bd_mask).astype(BF16)
    stack = lambda top, bottom: jnp.concatenate([top, bottom], axis=0).astype(BF16)

    chunk_sets = []
    for direction, (q_ref, k_ref, v_ref, ab_ref, o_ref) in enumerate(
            ((qf_ref, kf_ref, vf_ref, abf_ref, of_ref), (qb_ref, kb_ref, vb_ref, abb_ref, ob_ref))):
        sign = 1 - 2 * direction
        lag = (rows - cols) * sign
        for sub in range(DN_CHUNKS_PER_STEP):
            first_row = (sub if direction == 0 else DN_CHUNKS_PER_STEP - 1 - sub) * c
            rws = slice(first_row, first_row + c)
            ab = ab_ref[0, rws, :]
            a_exp = _dot_exact_rhs(ab, e_ref[direction, :, :DN_W])
            b_exp = _dot_exact_rhs(ab, e_ref[direction, :, DN_W:])
            chunk_sets.append(dict(
                q_ref=q_ref, k_ref=k_ref, v_ref=v_ref, o_ref=o_ref, direction=direction, sub=sub, rws=rws,
                incl=lag >= 0, strict=lag > 0, eye=jnp.where(lag == 0, 1.0, 0.0),
                tri=jnp.where((rr - cc) * sign >= 0, 1.0, 0.0).astype(BF16),
                tri_t=jnp.where((cols - rows) * sign >= 0, 1.0, 0.0),
                last=c - 1 if direction == 0 else 0,
                g=nega_ref[direction] * _softplus(a_exp + dtb_ref[direction]),
                beta=jax.nn.sigmoid(b_exp)))

    chains = []
    for sub in range(DN_CHUNKS_PER_STEP):
        for gi in range(n_groups):
            for cs in chunk_sets:
                if cs["sub"] != sub:
                    continue
                sl = slice(gi * gw, (gi + 1) * gw)
                rws = cs["rws"]
                chains.append(dict(cs, sl=sl, si=cs["direction"] * n_groups + gi,
                                   q=cs["q_ref"][0, rws, sl], k=cs["k_ref"][0, rws, sl],
                                   v=cs["v_ref"][0, rws, sl], g=cs["g"][:, sl], beta=cs["beta"][:, sl]))

    for ch in chains:
        ch["g_row"] = _dot_exact_lhs(ch["tri"], ch["g"])
    for ch in chains:
        ch["g_col"] = _dot_exact_lhs(ones8, ch["g"] * ch["tri_t"])[0:1]
        ch["g_tot"] = ch["g_row"][ch["last"]:ch["last"] + 1]
    for ch in chains:
        ch["kb"] = ch["k"] * ch["beta"]
        ch["kq"] = _dot_nt(stack(ch["kb"], ch["q"]), bd(ch["k"]))
    for ch in chains:
        incl = ch["incl"]
        decay = jnp.where(incl, jnp.exp(jnp.where(incl, ch["g_row"] - ch["g_col"], 0.0)), 0.0)
        ch["a_mat"] = jnp.where(ch["strict"], ch["kq"][:c] * decay, 0.0)
        ch["intra"] = jnp.where(incl, ch["kq"][c:] * decay, 0.0)
        ch["t_mat"] = ch["eye"] - ch["a_mat"]
    for ch in chains:
        ch["x"] = _dot(ch["a_mat"].astype(BF16), bd(ch["a_mat"]))
    for _ in range(4):
        for ch in chains:
            r = _dot(stack(ch["x"], ch["t_mat"]), bd(ch["x"]))
            ch["x"] = r[:c]
            ch["t_mat"] = ch["t_mat"] + r[c:]
    for ch in chains:
        ch["t_mat"] = (ch["t_mat"] + _dot(ch["t_mat"].astype(BF16), bd(ch["x"]))).astype(BF16)
    for ch in chains:
        ch["e_row"] = jnp.exp(ch["g_row"])
        ch["w_val"] = _dot(ch["t_mat"], bd(ch["v"] * ch["beta"]))
        ch["u_key"] = _dot(ch["t_mat"], bd(ch["kb"] * ch["e_row"]))
    n_states = 2 * n_groups
    state = [s_ref[si] for si in range(n_states)]
    for sub in range(DN_CHUNKS_PER_STEP):
        step = [ch for ch in chains if ch["sub"] == sub]
        for ch in step:
            ch["r2"] = _dot(stack(ch["u_key"], ch["q"] * ch["e_row"]), state[ch["si"]].astype(BF16))
        for ch in step:
            ch["v_new"] = ch["w_val"] - ch["r2"][:c]
            ch["o"] = ch["r2"][c:] + _dot(ch["intra"].astype(BF16), bd(ch["v_new"]))
            k_dec = ch["k"] * jnp.exp(ch["g_tot"] - ch["g_row"])
            ch["upd"] = _dot_tn(k_dec.astype(BF16), ch["v_new"].astype(BF16))
        for ch in step:
            state[ch["si"]] = state[ch["si"]] * jnp.exp(ch["g_tot"]) + jnp.where(bd_mask, ch["upd"], 0.0)
            ch["o_ref"][0, ch["rws"], ch["sl"]] = ch["o"]
    for si in range(n_states):
        s_ref[si] = state[si]


def _dn_scan(qkvn, ab, e_ab, neg_a, dt_b, n_ctx):
    b, s, _ = qkvn.shape
    c = DN_CHUNK * DN_CHUNKS_PER_STEP
    assert n_ctx % c == 0 and s % c == 0
    n_chunks = s // c
    ctx_chunks = n_ctx // c

    def fwd(t):
        return t

    def bwd(t):
        return jnp.where(t < ctx_chunks, ctx_chunks - 1 - t, n_chunks - 1 + ctx_chunks - t)

    def tok(order, part, width):
        return pl.BlockSpec((1, c, width), lambda i, t: (i, order(t), part))

    full = lambda arr: pl.BlockSpec(arr.shape, lambda i, t: (0,) * arr.ndim)
    out = jax.ShapeDtypeStruct((b, s, DN_W), F32)
    return pl.pallas_call(
        _dnscan_kernel,
        out_shape=[out, out],
        grid=(b, n_chunks),
        in_specs=[
            tok(fwd, 0, DN_W), tok(fwd, 1, DN_W), tok(fwd, 2, DN_W), tok(fwd, 0, AB_W),
            tok(bwd, 0, DN_W), tok(bwd, 1, DN_W), tok(bwd, 2, DN_W), tok(bwd, 0, AB_W),
            full(e_ab), full(neg_a), full(dt_b),
        ],
        out_specs=[tok(fwd, 0, DN_W), tok(bwd, 0, DN_W)],
        scratch_shapes=[pltpu.VMEM((2 * (DN_HEADS // DN_GROUP), DN_GW, DN_GW), F32)],
        compiler_params=_cp(("arbitrary", "arbitrary")),
        name="deltanet_scan",
    )(qkvn, qkvn, qkvn, ab, qkvn, qkvn, qkvn, ab, e_ab, neg_a, dt_b)


def _na_kernel(q_ref, k_ref, v_ref, qw_ref, kw_ref, bias_ref, o_ref, qm_ref, kn_ref, vm_ref,
               *, n_ctx, rows):
    lane = lax.broadcasted_iota(jnp.int32, (1, LANES), 1)
    head0 = lane < NA_DH
    ones = _same_group_ones(LANES, NA_DH)

    def head_norm(x, w):
        ms = _dot_exact_rhs(x * x, ones) * (1.0 / NA_DH)
        return x * lax.rsqrt(ms + EPS) * w

    q = head_norm(q_ref[0], qw_ref[...]) * NA_DH ** -0.5
    qm_ref[0] = jnp.where(head0, q, 0.0).astype(BF16)
    qm_ref[1] = jnp.where(head0, 0.0, q).astype(BF16)
    kn_ref[...] = head_norm(k_ref[0], kw_ref[...]).astype(BF16)
    v = v_ref[0]
    vm_ref[0] = jnp.where(head0, v, 0.0).astype(BF16)
    vm_ref[1] = jnp.where(head0, 0.0, v).astype(BF16)

    k_ctx = kn_ref[0:n_ctx, :]
    wr = WIN_R
    band = wr * GRID_W

    def attend(chains):
        for ch in chains:
            ch["s"] = [_dot_nt(ch["q"], k) if bias is None else _dot_nt(ch["q"], k) + bias
                       for k, _, bias in ch["keys"]]
        for ch in chains:
            m = functools.reduce(jnp.maximum, [jnp.max(sc, axis=-1, keepdims=True) for sc in ch["s"]])
            ch["p"] = [jnp.exp(sc - m) for sc in ch["s"]]
            ch["l"] = functools.reduce(jnp.add, [jnp.sum(p, axis=-1, keepdims=True) for p in ch["p"]])
        for ch in chains:
            ch["o"] = functools.reduce(jnp.add, [_dot(p.astype(BF16), v)
                                                 for p, (_, v, _) in zip(ch["p"], ch["keys"])])
        return [ch["o"] / ch["l"] for ch in chains]

    v_ctx = [vm_ref[h, 0:n_ctx, :] for h in range(2)]

    n_blk = n_ctx // GRID_W
    outs = attend([dict(q=qm_ref[h, blk * GRID_W:(blk + 1) * GRID_W, :], keys=[(k_ctx, v_ctx[h], None)])
                   for blk in range(n_blk) for h in range(2)])
    for blk in range(n_blk):
        o_ref[0, blk * GRID_W:(blk + 1) * GRID_W, :] = outs[2 * blk] + outs[2 * blk + 1]

    def rows_body(it, carry):
        chains, q0s = [], []
        for j in range(NA_ROWS_PER_STEP):
            r = it * NA_ROWS_PER_STEP + j
            rs = jnp.clip(r - wr // 2, 0, rows - wr)
            var = rs - r + (WIN_R - 1)
            q0 = pl.multiple_of(n_ctx + r * GRID_W, GRID_W)
            k0 = pl.multiple_of(n_ctx + rs * GRID_W, GRID_W)
            k_band = kn_ref[pl.ds(k0, band), :]
            q0s.append(q0)
            for h in range(2):
                chains.append(dict(q=qm_ref[h, pl.ds(q0, GRID_W), :],
                                   keys=[(k_band, vm_ref[h, pl.ds(k0, band), :], bias_ref[h, var]),
                                         (k_ctx, v_ctx[h], None)]))
        outs = attend(chains)
        for j, q0 in enumerate(q0s):
            o_ref[0, pl.ds(q0, GRID_W), :] = outs[2 * j] + outs[2 * j + 1]
        return carry

    lax.fori_loop(0, rows // NA_ROWS_PER_STEP, rows_body, 0)


def _na_attention(nqkv, q_w, k_w, bias_tab, n_ctx):
    b, s, _ = nqkv.shape
    rows = (s - n_ctx) // GRID_W
    pairs = NA_W // LANES
    tok = lambda part: pl.BlockSpec((1, s, LANES), lambda i, p: (i, 0, part * pairs + p))
    return pl.pallas_call(
        functools.partial(_na_kernel, n_ctx=n_ctx, rows=rows),
        out_shape=jax.ShapeDtypeStruct((b, s, NA_W), F32),
        grid=(b, pairs),
        in_specs=[
            tok(0), tok(1), tok(2),
            pl.BlockSpec((1, LANES), lambda i, p: (0, 0)),
            pl.BlockSpec((1, LANES), lambda i, p: (0, 0)),
            pl.BlockSpec((2, WIN_R, GRID_W, WIN_R * GRID_W), lambda i, p: (p, 0, 0, 0)),
        ],
        out_specs=pl.BlockSpec((1, s, LANES), lambda i, p: (i, 0, p)),
        scratch_shapes=[pltpu.VMEM((2, s, LANES), BF16), pltpu.VMEM((s, LANES), BF16),
                        pltpu.VMEM((2, s, LANES), BF16)],
        compiler_params=_cp(("arbitrary", "arbitrary")),
        name="neighbourhood_attention",
    )(nqkv, nqkv, nqkv, jnp.tile(q_w, 2).reshape(1, LANES), jnp.tile(k_w, 2).reshape(1, LANES),
      bias_tab)


def _merge_kernel(x_ref, odf_ref, odb_ref, z_ref, ona_ref, ga_ref, gb_ref, mod_ref, dnw_ref, n2w_ref,
                  ones_ref, wpa_ref, wpb_ref, wout_ref, wq_ref, xo_ref, h2_ref, pq_ref):
    o_dn = odf_ref[0] + odb_ref[0]
    parts = []
    for j in range(DN_W // LANES):
        oj = o_dn[:, j * LANES:(j + 1) * LANES]
        ms = _dot_exact_rhs(oj * oj, ones_ref[...]) * (1.0 / DN_DK)
        parts.append(oj * lax.rsqrt(ms + EPS))
    o_a = jnp.concatenate(parts, axis=1) * dnw_ref[...] * _silu(z_ref[0])
    y = (jax.nn.sigmoid(ga_ref[0]) * _dot(o_a.astype(BF16), wpa_ref[...])
         + jax.nn.sigmoid(gb_ref[0]) * _dot(ona_ref[0].astype(BF16), wpb_ref[...]))
    x = x_ref[0] + mod_ref[0, 0, 2:3, :] * _dot(y.astype(BF16), wout_ref[...])
    xo_ref[0] = x
    ms = jnp.mean(x * x, axis=-1, keepdims=True)
    h2 = (x * lax.rsqrt(ms + EPS) * n2w_ref[...]) * (1.0 + mod_ref[0, 0, 4:5, :]) + mod_ref[0, 0, 3:4, :]
    h2b = h2.astype(BF16)
    h2_ref[0] = h2b
    pq = _dot(h2b, wq_ref[...]).astype(BF16)
    for hp in range(pq_ref.shape[0]):
        pq_ref[hp, 0] = pq[:, hp * PEER_DK:(hp + 1) * PEER_DK]


def _merge(stream, o_dn_f, o_dn_b, z, o_na, ga, gb, modsel, dn_norm_w, norm2_w, w_pa, w_pb, w_out, w_q):
    b, s, d = stream.shape
    tm = TOK_TILE
    qw = w_q.shape[1]
    tok = lambda width: pl.BlockSpec((1, tm, width), lambda i, t: (i, t, 0))
    full = lambda arr: pl.BlockSpec(arr.shape, lambda i, t: (0,) * arr.ndim)
    dnw = jnp.tile(dn_norm_w, DN_HEADS).reshape(1, DN_W)
    n2w = norm2_w.reshape(1, d)
    ones = jnp.asarray(np.kron(np.eye(LANES // DN_DK), np.ones((DN_DK, DN_DK))), BF16)
    return pl.pallas_call(
        _merge_kernel,
        out_shape=[jax.ShapeDtypeStruct((b, s, d), F32), jax.ShapeDtypeStruct((b, s, d), BF16),
                   jax.ShapeDtypeStruct((qw // PEER_DK, b, s, PEER_DK), BF16)],
        grid=(b, s // tm),
        in_specs=[
            tok(d), tok(DN_W), tok(DN_W),
            tok(DN_W), tok(NA_W), tok(d), tok(d),
            pl.BlockSpec((1, 1, 8, d), lambda i, t: (i, jnp.minimum(t, 1), 0, 0)),
            full(dnw), full(n2w), full(ones), full(w_pa), full(w_pb), full(w_out), full(w_q),
        ],
        out_specs=[tok(d), tok(d),
                   pl.BlockSpec((qw // PEER_DK, 1, tm, PEER_DK), lambda i, t: (0, i, t, 0))],
        compiler_params=_cp(("arbitrary", "arbitrary")),
        name="merge_projection",
    )(stream, o_dn_f, o_dn_b, z, o_na, ga, gb, modsel, dnw, n2w, ones, w_pa, w_pb, w_out, w_q)


_LOG2E = 1.4426950408889634
_GELU_K1 = -2.0 * 0.7978845608028654 * _LOG2E
_GELU_K3 = _GELU_K1 * 0.044715


def _gelu_tanh(x):
    return x / (1.0 + jnp.exp2(x * (_GELU_K1 + _GELU_K3 * (x * x))))


def _peer_candidates(t1, t2, fill):
    sub = lax.broadcasted_iota(jnp.int32, (8, LANES), 0)
    cands = [t1[0:1] + t2[0:8], t1[0:1] + t2[8:16], t1[1:2] + t2[0:8]]
    for i in range(2, 8):
        cands.append(jnp.where(sub < PEER_TOPK // (i + 1), t1[i:i + 1] + t2[0:8], fill))
    cands.append(t1[8:16] + t2[0:1])
    return cands


def _peer_select_chunk(s1, s2, t1_ref, t2_ref):
    neg = jnp.float32(-jnp.inf)
    for t_ref, s in ((t1_ref, s1), (t2_ref, s2)):
        cur = s
        for kk in range(PEER_TOPK + 1):
            m = jnp.max(cur, axis=0, keepdims=True)
            t_ref[kk:kk + 1, :] = m
            if kk < PEER_TOPK:
                cur = jnp.where(cur == m, neg, cur)
    t1 = t1_ref[0:PEER_TOPK, :]
    t2 = t2_ref[0:PEER_TOPK, :]
    t1_next = t1_ref[PEER_TOPK:PEER_TOPK + 1, :]
    t2_next = t2_ref[PEER_TOPK:PEER_TOPK + 1, :]
    cands = _peer_candidates(t1, t2, neg)
    cur = cands
    tau = None
    for kk in range(PEER_TOPK):
        m = jnp.max(functools.reduce(jnp.maximum, cur), axis=0, keepdims=True)
        if kk + 1 < PEER_TOPK:
            cur = [jnp.where(cd == m, neg, cd) for cd in cur]
        else:
            tau = m
    m1 = t1[0:1]
    m2 = t2[0:1]
    top = m1 + m2
    z = functools.reduce(jnp.add, [
        jnp.sum(jnp.where(cd >= tau, jnp.exp(cd - top), 0.0), axis=0, keepdims=True) for cd in cands])
    log2_z = jnp.log(z) * _LOG2E
    to_p1 = lambda s: (s - m1) * _LOG2E - log2_z
    to_p2 = lambda s: (s - m2) * _LOG2E
    cands_p = _peer_candidates(to_p1(t1), to_p2(t2), neg)
    low = functools.reduce(jnp.minimum, [
        jnp.min(jnp.where(cd >= tau, cp, jnp.inf), axis=0, keepdims=True) for cd, cp in zip(cands, cands_p)])
    below = functools.reduce(jnp.maximum, [
        jnp.max(jnp.where(cd >= tau, neg, cp), axis=0, keepdims=True) for cd, cp in zip(cands, cands_p)])
    below = jnp.maximum(below, jnp.maximum(to_p1(t1[0:1]) + to_p2(t2_next), to_p1(t1_next) + to_p2(t2[0:1])))
    return to_p1(s1), to_p2(s2), 0.5 * (low + below)


def _peer_kernel(x_ref, h2_ref, pq_ref, g2_ref, keys_ref, u_ref, vt_ref, o_ref,
                 s_ref, th_ref, e1_ref, p2_ref, e2_ref, t1_ref, t2_ref, act_ref, g_ref, acc_ref, *, sub_tile):
    e = pl.program_id(1)
    tb = h2_ref.shape[0]
    n_chunk = tb // LANES
    eb = u_ref.shape[0]

    @pl.when(e == 0)
    def _():
        acc_ref[...] = jnp.zeros_like(acc_ref)
        for hp in range(2 * PEER_HEADS):
            st = _dot_nt(keys_ref[hp], pq_ref[hp])
            for ci in range(n_chunk):
                s_ref[hp, ci] = st[:, ci * LANES:(ci + 1) * LANES]

        def select_body(h, carry):
            for ci in range(n_chunk):
                p1, p2, thr = _peer_select_chunk(s_ref[2 * h, ci], s_ref[2 * h + 1, ci],
                                                 t1_ref.at[ci], t2_ref.at[ci])
                th_ref[h, ci] = thr - p1
                e1_ref[h, ci] = jnp.exp2(p1)
                p2_ref[h, ci] = p2
                e2_ref[h, ci] = jnp.exp2(p2)
            return carry

        lax.fori_loop(0, PEER_HEADS, select_body, 0)

    act_ref[...] = _gelu_tanh(_dot_nt(u_ref[...], h2_ref[...]))

    def gate_body(al, carry):
        a = e * (eb // N_KEYS) + al
        r0 = pl.multiple_of(al * N_KEYS, N_KEYS)
        for ci in range(n_chunk):
            cols = slice(ci * LANES, (ci + 1) * LANES)
            w = jnp.zeros((N_KEYS, LANES), F32)
            for h in range(PEER_HEADS):
                picked = p2_ref[h, ci] >= th_ref[h, ci, pl.ds(a, 1), :]
                w = w + jnp.where(picked, e2_ref[h, ci] * e1_ref[h, ci, pl.ds(a, 1), :], 0.0)
            g_ref[pl.ds(r0, N_KEYS), cols] = (w * act_ref[pl.ds(r0, N_KEYS), cols]).astype(BF16)
        return carry

    lax.fori_loop(0, eb // N_KEYS, gate_body, 0)
    acc_ref[...] += _dot(vt_ref[...], g_ref[...])

    @pl.when(e == pl.num_programs(1) - 1)
    def _():
        f = acc_ref[...].T
        for si in range(tb // sub_tile):
            rows = slice(si * sub_tile, (si + 1) * sub_tile)
            o_ref[rows, :] = x_ref[rows, :] + g2_ref[si] * f[rows, :]


def _peer(x_flat, h2_flat, pq_heads, g2_sub, keys_b, u_all, vt_all, layer):
    t, d = x_flat.shape
    tb = PEER_TB
    eb = PEER_EB
    n_exp = u_all.shape[1]
    n_blocks = n_exp // eb
    n_chunk = tb // LANES
    n_sub = tb // TOK_TILE
    tables = pltpu.VMEM((PEER_HEADS, n_chunk, N_KEYS, LANES), F32)
    return pl.pallas_call(
        functools.partial(_peer_kernel, sub_tile=TOK_TILE),
        out_shape=jax.ShapeDtypeStruct((t, d), F32),
        grid=(t // tb, n_blocks),
        in_specs=[
            pl.BlockSpec((tb, d), lambda i, e: (i, 0)),
            pl.BlockSpec((tb, d), lambda i, e: (i, 0)),
            pl.BlockSpec((2 * PEER_HEADS, tb, PEER_DK), lambda i, e: (0, i, 0)),
            pl.BlockSpec((n_sub, 1, d), lambda i, e: (i, 0, 0)),
            pl.BlockSpec(keys_b.shape, lambda i, e: (0, 0, 0)),
            pl.BlockSpec((None, eb, d), lambda i, e: (layer, e, 0)),
            pl.BlockSpec((None, d, eb), lambda i, e: (layer, 0, e)),
        ],
        out_specs=pl.BlockSpec((tb, d), lambda i, e: (i, 0)),
        scratch_shapes=[
            pltpu.VMEM((2 * PEER_HEADS, n_chunk, N_KEYS, LANES), F32),
            tables,
            tables,
            tables,
            tables,
            pltpu.VMEM((n_chunk, PEER_TOPK + 8, LANES), F32),
            pltpu.VMEM((n_chunk, PEER_TOPK + 8, LANES), F32),
            pltpu.VMEM((eb, tb), F32),
            pltpu.VMEM((eb, tb), BF16),
            pltpu.VMEM((d, tb), F32),
        ],
        compiler_params=_cp(("arbitrary", "arbitrary")),
        name="peer_dense",
    )(x_flat, h2_flat, pq_heads, g2_sub, keys_b, u_all, vt_all)


def _static_tables(n_ctx, n_lat):
    n_freq = DN_DK // 4
    t = np.arange(n_lat)
    row = (t // GRID_W).astype(np.float32)
    col = (t % GRID_W).astype(np.float32)
    inv_freq = (np.float32(ROPE_BASE) ** (-np.arange(n_freq, dtype=np.float32) / n_freq)).astype(np.float32)
    ang = np.concatenate([row[:, None] * inv_freq, col[:, None] * inv_freq], axis=-1)
    cos_h = np.concatenate([np.cos(ang), np.cos(ang)], axis=-1)
    sin_h = np.concatenate([-np.sin(ang), np.sin(ang)], axis=-1)
    cos_t = np.concatenate([np.ones((n_ctx, DN_DK), np.float32), cos_h], axis=0)
    sin_t = np.concatenate([np.zeros((n_ctx, DN_DK), np.float32), sin_h], axis=0)
    cos_t = jnp.asarray(np.tile(cos_t, (1, LANES // DN_DK)), F32)
    sin_t = jnp.asarray(np.tile(sin_t, (1, LANES // DN_DK)), F32)
    sel = np.zeros((2, 2, AB_W, DN_W), np.float32)
    for which in range(2):
        for dr in range(2):
            for h in range(DN_HEADS):
                sel[which, dr, which * 2 * DN_HEADS + dr * DN_HEADS + h, h * DN_DK:(h + 1) * DN_DK] = 1.0
    return cos_t, sin_t, jnp.asarray(np.concatenate([sel[0], sel[1]], axis=-1), BF16)


def _na_bias_table(rpb):
    c = np.arange(GRID_W)
    kc = np.arange(GRID_W)
    cs = np.clip(c - WIN_C // 2, 0, GRID_W - WIN_C)
    in_win = (kc[None, :] >= cs[:, None]) & (kc[None, :] < cs[:, None] + WIN_C)
    dc = np.clip(kc[None, :] - c[:, None] + (WIN_C - 1), 0, 2 * WIN_C - 2)
    var = np.arange(WIN_R)
    jj = np.arange(WIN_R)
    dr = var[:, None] + jj[None, :]
    tab = rpb[:, dr][:, :, :, dc]
    tab = jnp.where(in_win[None, None, None], tab, NEG_INF)
    tab = tab.transpose(0, 1, 3, 2, 4)
    return tab.reshape(rpb.shape[0], WIN_R, GRID_W, WIN_R * GRID_W)


def kernel(x, c, ctx, c_ctx, ada_w, ada_b, norm1_w, norm2_w, w_in, dn_conv_w, dn_a_log, dn_dt_bias,
           dn_norm_w, na_qnorm_w, na_knorm_w, na_rpb, w_pa, w_pb, w_out, peer_wq, peer_keys, peer_u,
           peer_v):
    b, n_lat, d = x.shape
    n_ctx = ctx.shape[1]
    depth = ada_w.shape[0]
    s = n_ctx + n_lat
    assert n_ctx % TOK_TILE == 0 and n_lat % TOK_TILE == 0 and (b * s) % PEER_TB == 0
    assert n_lat % GRID_W == 0 and n_lat // GRID_W >= WIN_R and d == 1024

    stream = jnp.concatenate([ctx, x], axis=1)

    cc = jnp.zeros((16, d), F32).at[:b].set(c).at[b].set(c_ctx)
    mod = _modulation(cc, ada_w, ada_b).reshape(depth, 16, N_MOD, d)
    mod = jnp.pad(mod, ((0, 0), (0, 0), (0, 8 - N_MOD), (0, 0)))
    modsel = jnp.stack([jnp.broadcast_to(mod[:, b][:, None], (depth, b, 8, d)), mod[:, :b]], axis=2)

    cos_t, sin_t, e_ab = _static_tables(n_ctx, n_lat)
    u_all = peer_u.astype(BF16)
    vt_all = jnp.swapaxes(peer_v, 1, 2).astype(BF16)

    for l in range(depth):
        wl = w_in[l]
        o1 = 3 * DN_W + DN_W
        o2 = o1 + 4 * DN_HEADS
        w_in_p = jnp.concatenate(
            [wl[:, :o1], wl[:, o1:o2], jnp.zeros((d, AB_W - 4 * DN_HEADS), F32), wl[:, o2:]],
            axis=1).astype(BF16)
        dqkv, z, ab, nqkv, ga, gb = _in_projection(stream, modsel[l], norm1_w[l], w_in_p)

        qkvn = _dn_prep(dqkv, dn_conv_w[l], cos_t, sin_t, n_ctx)
        neg_a = jnp.repeat(-jnp.exp(dn_a_log[l]), DN_DK, axis=-1).reshape(2, 1, DN_W)
        dt_b = jnp.repeat(dn_dt_bias[l], DN_DK, axis=-1).reshape(2, 1, DN_W)
        o_dn_f, o_dn_b = _dn_scan(qkvn, ab, e_ab, neg_a, dt_b, n_ctx)

        o_na = _na_attention(nqkv, na_qnorm_w[l], na_knorm_w[l], _na_bias_table(na_rpb[l]), n_ctx)

        stream, h2, pq = _merge(stream, o_dn_f, o_dn_b, z, o_na, ga, gb, modsel[l], dn_norm_w[l], norm2_w[l],
                                w_pa[l].astype(BF16), w_pb[l].astype(BF16), w_out[l].astype(BF16),
                                peer_wq[l].astype(BF16))

        g2_sub = jnp.broadcast_to(modsel[l][:, :, 5][:, jnp.minimum(jnp.arange(s // TOK_TILE), 1)],
                                  (b, s // TOK_TILE, d)).reshape(b * s // TOK_TILE, 1, d)
        keys_b = peer_keys[l].reshape(2 * PEER_HEADS, N_KEYS, PEER_DK).astype(BF16)
        out = _peer(stream.reshape(b * s, d), h2.reshape(b * s, d), pq.reshape(pq.shape[0], b * s, PEER_DK),
                    g2_sub, keys_b, u_all, vt_all, l)
        stream = out.reshape(b, s, d)

    return stream[:, n_ctx:]
```

```python
import functools

import numpy as np
import jax
import jax.numpy as jnp
from jax import lax
from jax.experimental import pallas as pl
from jax.experimental.pallas import tpu as pltpu

F32 = jnp.float32
BF16 = jnp.bfloat16

GRID_W = 64
DN_HEADS = 8
DN_DK = 64
DN_W = DN_HEADS * DN_DK
DN_CHUNK = 64
CONV_K = 5
NA_HEADS = 8
NA_DH = 64
NA_W = NA_HEADS * NA_DH
WIN_R = 8
WIN_C = 16
ROPE_BASE = 10000.0
PEER_HEADS = 8
N_KEYS = 128
PEER_DK = 128
PEER_TOPK = 16
N_MOD = 6
EPS = 1e-6
NEG_INF = -1e30

LANES = 128
TOK_TILE = 256
DN_GROUP = 4
DN_GW = DN_GROUP * DN_DK
DN_INV_BASE = 8
DN_CHUNKS_PER_STEP = 4
AB_W = 128
IN_W_PAD = 3 * DN_W + DN_W + AB_W + 3 * NA_W + 2 * 1024
NA_ROWS_PER_STEP = 8
PEER_TB = 512
PEER_EB = 1024
VMEM_LIMIT = 56 * 1024 * 1024


def _cp(sem, vmem=VMEM_LIMIT, flags=None):
    return pltpu.CompilerParams(dimension_semantics=sem, vmem_limit_bytes=vmem, flags=flags)


def _dot(a, b):
    return jnp.dot(a, b, preferred_element_type=F32)


def _dot_nt(a, b):
    return lax.dot_general(a, b, (((1,), (1,)), ((), ())), preferred_element_type=F32)


def _dot_tn(a, b):
    return lax.dot_general(a, b, (((0,), (0,)), ((), ())), preferred_element_type=F32)


def _split2(x):
    hi = x.astype(BF16)
    lo = (x - hi.astype(F32)).astype(BF16)
    return hi, lo


def _dot_exact_lhs(m_bf16, x):
    hi, lo = _split2(x)
    return _dot(m_bf16, hi) + _dot(m_bf16, lo)


def _dot_exact_rhs(x, m_bf16):
    hi, lo = _split2(x)
    return _dot(hi, m_bf16) + _dot(lo, m_bf16)


def _silu(x):
    return x * jax.nn.sigmoid(x)


def _softplus(x):
    return jnp.maximum(x, 0.0) + jnp.log1p(jnp.exp(-jnp.abs(x)))


def _log2(n):
    assert n & (n - 1) == 0
    return n.bit_length() - 1


def _same_group_ones(n, group):
    r = lax.broadcasted_iota(jnp.int32, (n, n), 0) >> _log2(group)
    c = lax.broadcasted_iota(jnp.int32, (n, n), 1) >> _log2(group)
    return jnp.where(r == c, 1.0, 0.0).astype(BF16)


def _mod_kernel(c_ref, w_ref, b_ref, o_ref):
    s = _silu(c_ref[...])
    o_ref[0] = _dot(s.astype(BF16), w_ref[0].astype(BF16)) + b_ref[0]


def _modulation(cc, ada_w, ada_b):
    depth, d, _ = ada_w.shape
    rows = cc.shape[0]
    return pl.pallas_call(
        _mod_kernel,
        out_shape=jax.ShapeDtypeStruct((depth, rows, N_MOD * d), F32),
        grid=(depth, N_MOD),
        in_specs=[
            pl.BlockSpec((rows, d), lambda l, j: (0, 0)),
            pl.BlockSpec((1, d, d), lambda l, j: (l, 0, j)),
            pl.BlockSpec((1, 1, d), lambda l, j: (l, 0, j)),
        ],
        out_specs=pl.BlockSpec((1, rows, d), lambda l, j: (l, 0, j)),
        compiler_params=_cp(("arbitrary", "arbitrary")),
        name="adaln_modulation",
    )(cc, ada_w, ada_b.reshape(depth, 1, N_MOD * d))


_IN_SPLITS = (3 * DN_W, DN_W, AB_W, 3 * NA_W, 1024, 1024)


def _inproj_kernel(x_ref, mod_ref, nw_ref, w_ref, *out_refs):
    x = x_ref[0]
    ms = jnp.mean(x * x, axis=-1, keepdims=True)
    y = x * lax.rsqrt(ms + EPS) * nw_ref[...]
    h = y * (1.0 + mod_ref[0, 0, 1:2, :]) + mod_ref[0, 0, 0:1, :]
    hb = h.astype(BF16)
    off = 0
    for o_ref, width in zip(out_refs, _IN_SPLITS):
        o_ref[0] = _dot(hb, w_ref[:, off:off + width])
        off += width


def _in_projection(stream, modsel, norm_w, w_in_p):
    b, s, d = stream.shape
    tm = TOK_TILE
    tok = lambda width: pl.BlockSpec((1, tm, width), lambda i, t: (i, t, 0))
    return pl.pallas_call(
        _inproj_kernel,
        out_shape=[jax.ShapeDtypeStruct((b, s, width), F32) for width in _IN_SPLITS],
        grid=(b, s // tm),
        in_specs=[
            tok(d),
            pl.BlockSpec((1, 1, 8, d), lambda i, t: (i, jnp.minimum(t, 1), 0, 0)),
            pl.BlockSpec((1, d), lambda i, t: (0, 0)),
            pl.BlockSpec((d, IN_W_PAD), lambda i, t: (0, 0)),
        ],
        out_specs=[tok(width) for width in _IN_SPLITS],
        compiler_params=_cp(("arbitrary", "arbitrary")),
        name="norm_in_projection",
    )(stream, modsel, norm_w.reshape(1, d), w_in_p)


def _dnprep_kernel(x_ref, cw_ref, cos_ref, sin_ref, o_ref, *, n_ctx):
    j = pl.program_id(1)
    x = x_ref[0]
    s = x.shape[0]
    t = lax.broadcasted_iota(jnp.int32, x.shape, 0)
    in_ctx = t < n_ctx
    pos = jnp.where(in_ctx, t, t - n_ctx)
    seg_len = jnp.where(in_ctx, n_ctx, s - n_ctx)
    y = jnp.zeros_like(x)
    for tap in range(CONV_K):
        d = tap - CONV_K // 2
        xs = x if d == 0 else pltpu.roll(x, (s - d) % s, 0)
        ok = (pos + d >= 0) & (pos + d < seg_len)
        y = y + jnp.where(ok, xs, 0.0) * cw_ref[tap:tap + 1, :]
    y = _silu(y)

    @pl.when(j < 2 * (DN_W // LANES))
    def _():
        ones = _same_group_ones(LANES, DN_DK)
        ss = _dot_exact_rhs(y * y, ones)
        yn = y * lax.rsqrt(ss + EPS)
        lane = lax.broadcasted_iota(jnp.int32, x.shape, 1)
        first_half = (lane & (DN_DK - 1)) < DN_DK // 2
        partner = jnp.where(first_half,
                            pltpu.roll(yn, LANES - DN_DK // 2, 1),
                            pltpu.roll(yn, DN_DK // 2, 1))
        out = yn * cos_ref[...] + partner * sin_ref[...]
        qscale = jnp.where(j < DN_W // LANES, DN_DK ** -0.5, 1.0)
        o_ref[0] = out * qscale

    @pl.when(j >= 2 * (DN_W // LANES))
    def _():
        o_ref[0] = y


def _dn_prep(dqkv, conv_w, cos_t, sin_t, n_ctx):
    b, s, w = dqkv.shape
    return pl.pallas_call(
        functools.partial(_dnprep_kernel, n_ctx=n_ctx),
        out_shape=jax.ShapeDtypeStruct((b, s, w), F32),
        grid=(b, w // LANES),
        in_specs=[
            pl.BlockSpec((1, s, LANES), lambda i, j: (i, 0, j)),
            pl.BlockSpec((CONV_K, LANES), lambda i, j: (0, j)),
            pl.BlockSpec((s, LANES), lambda i, j: (0, 0)),
            pl.BlockSpec((s, LANES), lambda i, j: (0, 0)),
        ],
        out_specs=pl.BlockSpec((1, s, LANES), lambda i, j: (i, 0, j)),
        compiler_params=_cp(("arbitrary", "arbitrary")),
        name="deltanet_prep",
    )(dqkv, conv_w, cos_t, sin_t)


def _tile_rows_bd(x, bd_mask):
    return jnp.where(bd_mask, jnp.concatenate([x] * DN_GROUP, axis=0), 0.0)


def _dnscan_kernel(qf_ref, kf_ref, vf_ref, abf_ref, qb_ref, kb_ref, vb_ref, abb_ref,
                   e_ref, nega_ref, dtb_ref, of_ref, ob_ref, s_ref):
    c = DN_CHUNK
    gw = DN_GW
    n_groups = DN_HEADS // DN_GROUP

    @pl.when(pl.program_id(1) == 0)
    def _():
        s_ref[...] = jnp.zeros_like(s_ref)

    rows = lax.broadcasted_iota(jnp.int32, (c, gw), 0)
    cols = lax.broadcasted_iota(jnp.int32, (c, gw), 1) & (c - 1)
    rr = lax.broadcasted_iota(jnp.int32, (c, c), 0)
    cc = lax.broadcasted_iota(jnp.int32, (c, c), 1)
    ones8 = jnp.ones((8, c), BF16)
    bd_mask = (lax.broadcasted_iota(jnp.int32, (gw, gw), 0) >> _log2(c)) == (
        lax.broadcasted_iota(jnp.int32, (gw, gw), 1) >> _log2(c))
    bd = lambda x: _tile_rows_bd(x, bd_mask).astype(BF16)
    stack = lambda top, bottom: jnp.concatenate([top, bottom], axis=0).astype(BF16)

    chunk_sets = []
    for direction, (q_ref, k_ref, v_ref, ab_ref, o_ref) in enumerate(
            ((qf_ref, kf_ref, vf_ref, abf_ref, of_ref), (qb_ref, kb_ref, vb_ref, abb_ref, ob_ref))):
        sign = 1 - 2 * direction
        lag = (rows - cols) * sign
        for sub in range(DN_CHUNKS_PER_STEP):
            first_row = (sub if direction == 0 else DN_CHUNKS_PER_STEP - 1 - sub) * c
            rws = slice(first_row, first_row + c)
            ab = ab_ref[0, rws, :]
            a_exp = _dot_exact_rhs(ab, e_ref[direction, :, :DN_W])
            b_exp = _dot_exact_rhs(ab, e_ref[direction, :, DN_W:])
            chunk_sets.append(dict(
                q_ref=q_ref, k_ref=k_ref, v_ref=v_ref, o_ref=o_ref, direction=direction, sub=sub, rws=rws,
                incl=lag >= 0, strict=lag > 0, eye=jnp.where(lag == 0, 1.0, 0.0),
                tri=jnp.where((rr - cc) * sign >= 0, 1.0, 0.0).astype(BF16),
                tri_t=jnp.where((cols - rows) * sign >= 0, 1.0, 0.0),
                last=c - 1 if direction == 0 else 0,
                g=nega_ref[direction] * _softplus(a_exp + dtb_ref[direction]),
                beta=jax.nn.sigmoid(b_exp)))

    chains = []
    for sub in range(DN_CHUNKS_PER_STEP):
        for gi in range(n_groups):
            for cs in chunk_sets:
                if cs["sub"] != sub:
                    continue
                sl = slice(gi * gw, (gi + 1) * gw)
                rws = cs["rws"]
                chains.append(dict(cs, sl=sl, si=cs["direction"] * n_groups + gi,
                                   q=cs["q_ref"][0, rws, sl], k=cs["k_ref"][0, rws, sl],
                                   v=cs["v_ref"][0, rws, sl], g=cs["g"][:, sl], beta=cs["beta"][:, sl]))

    for ch in chains:
        ch["g_row"] = _dot_exact_lhs(ch["tri"], ch["g"])
    for ch in chains:
        ch["g_col"] = _dot_exact_lhs(ones8, ch["g"] * ch["tri_t"])[0:1]
        ch["g_tot"] = ch["g_row"][ch["last"]:ch["last"] + 1]
    for ch in chains:
        ch["kb"] = ch["k"] * ch["beta"]
        ch["kq"] = _dot_nt(stack(ch["kb"], ch["q"]), bd(ch["k"]))
    for ch in chains:
        incl = ch["incl"]
        decay = jnp.where(incl, jnp.exp(jnp.where(incl, ch["g_row"] - ch["g_col"], 0.0)), 0.0)
        ch["a_mat"] = jnp.where(ch["strict"], ch["kq"][:c] * decay, 0.0)
        ch["intra"] = jnp.where(incl, ch["kq"][c:] * decay, 0.0)
    blk = lambda size: (rows >> _log2(size)) == (cols >> _log2(size))
    for ch in chains:
        ch["x"] = jnp.where(blk(DN_INV_BASE), -ch["a_mat"], 0.0)
        ch["t_mat"] = ch["eye"] + ch["x"]
    for ch in chains:
        ch["x"] = _dot(ch["x"].astype(BF16), bd(ch["x"]))
    for ch in chains:
        r = _dot(stack(ch["x"], ch["t_mat"]), bd(ch["x"]))
        ch["x"] = r[:c]
        ch["t_mat"] = ch["t_mat"] + r[c:]
    for ch in chains:
        ch["t_mat"] = ch["t_mat"] + _dot(ch["t_mat"].astype(BF16), bd(ch["x"]))
    size = DN_INV_BASE
    while size < c:
        for ch in chains:
            off = jnp.where(blk(2 * size) & jnp.logical_not(blk(size)), ch["a_mat"], 0.0)
            ch["x"] = _dot(off.astype(BF16), bd(ch["t_mat"]))
        for ch in chains:
            ch["t_mat"] = ch["t_mat"] - _dot(ch["t_mat"].astype(BF16), bd(ch["x"]))
        size *= 2
    for ch in chains:
        ch["t_mat"] = ch["t_mat"].astype(BF16)
    for ch in chains:
        ch["e_row"] = jnp.exp(ch["g_row"])
        ch["w_val"] = _dot(ch["t_mat"], bd(ch["v"] * ch["beta"]))
        ch["u_key"] = _dot(ch["t_mat"], bd(ch["kb"] * ch["e_row"]))
    n_states = 2 * n_groups
    state = [s_ref[si] for si in range(n_states)]
    for sub in range(DN_CHUNKS_PER_STEP):
        step = [ch for ch in chains if ch["sub"] == sub]
        for ch in step:
            ch["r2"] = _dot(stack(ch["u_key"], ch["q"] * ch["e_row"]), state[ch["si"]].astype(BF16))
        for ch in step:
            ch["v_new"] = ch["w_val"] - ch["r2"][:c]
            ch["o"] = ch["r2"][c:] + _dot(ch["intra"].astype(BF16), bd(ch["v_new"]))
            k_dec = ch["k"] * jnp.exp(ch["g_tot"] - ch["g_row"])
            ch["upd"] = _dot_tn(k_dec.astype(BF16), ch["v_new"].astype(BF16))
        for ch in step:
            state[ch["si"]] = state[ch["si"]] * jnp.exp(ch["g_tot"]) + jnp.where(bd_mask, ch["upd"], 0.0)
            ch["o_ref"][0, ch["rws"], ch["sl"]] = ch["o"]
    for si in range(n_states):
        s_ref[si] = state[si]


def _dn_scan(qkvn, ab, e_ab, neg_a, dt_b, n_ctx):
    b, s, _ = qkvn.shape
    c = DN_CHUNK * DN_CHUNKS_PER_STEP
    assert n_ctx % c == 0 and s % c == 0
    n_chunks = s // c
    ctx_chunks = n_ctx // c

    def fwd(t):
        return t

    def bwd(t):
        return jnp.where(t < ctx_chunks, ctx_chunks - 1 - t, n_chunks - 1 + ctx_chunks - t)

    def tok(order, part, width):
        return pl.BlockSpec((1, c, width), lambda i, t: (i, order(t), part))

    full = lambda arr: pl.BlockSpec(arr.shape, lambda i, t: (0,) * arr.ndim)
    out = jax.ShapeDtypeStruct((b, s, DN_W), F32)
    return pl.pallas_call(
        _dnscan_kernel,
        out_shape=[out, out],
        grid=(b, n_chunks),
        in_specs=[
            tok(fwd, 0, DN_W), tok(fwd, 1, DN_W), tok(fwd, 2, DN_W), tok(fwd, 0, AB_W),
            tok(bwd, 0, DN_W), tok(bwd, 1, DN_W), tok(bwd, 2, DN_W), tok(bwd, 0, AB_W),
            full(e_ab), full(neg_a), full(dt_b),
        ],
        out_specs=[tok(fwd, 0, DN_W), tok(bwd, 0, DN_W)],
        scratch_shapes=[pltpu.VMEM((2 * (DN_HEADS // DN_GROUP), DN_GW, DN_GW), F32)],
        compiler_params=_cp(("arbitrary", "arbitrary")),
        name="deltanet_scan",
    )(qkvn, qkvn, qkvn, ab, qkvn, qkvn, qkvn, ab, e_ab, neg_a, dt_b)


def _na_kernel(q_ref, k_ref, v_ref, qw_ref, kw_ref, bias_ref, o_ref, qm_ref, kn_ref, vm_ref,
               *, n_ctx, rows):
    lane = lax.broadcasted_iota(jnp.int32, (1, LANES), 1)
    head0 = lane < NA_DH
    ones = _same_group_ones(LANES, NA_DH)

    def head_norm(x, w):
        ms = _dot_exact_rhs(x * x, ones) * (1.0 / NA_DH)
        return x * lax.rsqrt(ms + EPS) * w

    q = head_norm(q_ref[0], qw_ref[...]) * NA_DH ** -0.5
    qm_ref[0] = jnp.where(head0, q, 0.0).astype(BF16)
    qm_ref[1] = jnp.where(head0, 0.0, q).astype(BF16)
    kn_ref[...] = head_norm(k_ref[0], kw_ref[...]).astype(BF16)
    v = v_ref[0]
    vm_ref[0] = jnp.where(head0, v, 0.0).astype(BF16)
    vm_ref[1] = jnp.where(head0, 0.0, v).astype(BF16)

    k_ctx = kn_ref[0:n_ctx, :]
    wr = WIN_R
    band = wr * GRID_W

    def attend(chains):
        for ch in chains:
            ch["s"] = [_dot_nt(ch["q"], k) if bias is None else _dot_nt(ch["q"], k) + bias
                       for k, _, bias in ch["keys"]]
        for ch in chains:
            m = functools.reduce(jnp.maximum, [jnp.max(sc, axis=-1, keepdims=True) for sc in ch["s"]])
            ch["p"] = [jnp.exp(sc - m) for sc in ch["s"]]
            ch["l"] = functools.reduce(jnp.add, [jnp.sum(p, axis=-1, keepdims=True) for p in ch["p"]])
        for ch in chains:
            ch["o"] = functools.reduce(jnp.add, [_dot(p.astype(BF16), v)
                                                 for p, (_, v, _) in zip(ch["p"], ch["keys"])])
        return [ch["o"] / ch["l"] for ch in chains]

    v_ctx = [vm_ref[h, 0:n_ctx, :] for h in range(2)]

    n_blk = n_ctx // GRID_W
    outs = attend([dict(q=qm_ref[h, blk * GRID_W:(blk + 1) * GRID_W, :], keys=[(k_ctx, v_ctx[h], None)])
                   for blk in range(n_blk) for h in range(2)])
    for blk in range(n_blk):
        o_ref[0, blk * GRID_W:(blk + 1) * GRID_W, :] = outs[2 * blk] + outs[2 * blk + 1]

    def rows_body(it, carry):
        chains, q0s = [], []
        for j in range(NA_ROWS_PER_STEP):
            r = it * NA_ROWS_PER_STEP + j
            rs = jnp.clip(r - wr // 2, 0, rows - wr)
            var = rs - r + (WIN_R - 1)
            q0 = pl.multiple_of(n_ctx + r * GRID_W, GRID_W)
            k0 = pl.multiple_of(n_ctx + rs * GRID_W, GRID_W)
            k_band = kn_ref[pl.ds(k0, band), :]
            q0s.append(q0)
            for h in range(2):
                chains.append(dict(q=qm_ref[h, pl.ds(q0, GRID_W), :],
                                   keys=[(k_band, vm_ref[h, pl.ds(k0, band), :], bias_ref[h, var]),
                                         (k_ctx, v_ctx[h], None)]))
        outs = attend(chains)
        for j, q0 in enumerate(q0s):
            o_ref[0, pl.ds(q0, GRID_W), :] = outs[2 * j] + outs[2 * j + 1]
        return carry

    lax.fori_loop(0, rows // NA_ROWS_PER_STEP, rows_body, 0)


def _na_attention(nqkv, q_w, k_w, bias_tab, n_ctx):
    b, s, _ = nqkv.shape
    rows = (s - n_ctx) // GRID_W
    pairs = NA_W // LANES
    tok = lambda part: pl.BlockSpec((1, s, LANES), lambda i, p: (i, 0, part * pairs + p))
    return pl.pallas_call(
        functools.partial(_na_kernel, n_ctx=n_ctx, rows=rows),
        out_shape=jax.ShapeDtypeStruct((b, s, NA_W), F32),
        grid=(b, pairs),
        in_specs=[
            tok(0), tok(1), tok(2),
            pl.BlockSpec((1, LANES), lambda i, p: (0, 0)),
            pl.BlockSpec((1, LANES), lambda i, p: (0, 0)),
            pl.BlockSpec((2, WIN_R, GRID_W, WIN_R * GRID_W), lambda i, p: (p, 0, 0, 0)),
        ],
        out_specs=pl.BlockSpec((1, s, LANES), lambda i, p: (i, 0, p)),
        scratch_shapes=[pltpu.VMEM((2, s, LANES), BF16), pltpu.VMEM((s, LANES), BF16),
                        pltpu.VMEM((2, s, LANES), BF16)],
        compiler_params=_cp(("arbitrary", "arbitrary")),
        name="neighbourhood_attention",
    )(nqkv, nqkv, nqkv, jnp.tile(q_w, 2).reshape(1, LANES), jnp.tile(k_w, 2).reshape(1, LANES),
      bias_tab)


def _merge_kernel(x_ref, odf_ref, odb_ref, z_ref, ona_ref, ga_ref, gb_ref, mod_ref, dnw_ref, n2w_ref,
                  ones_ref, wpa_ref, wpb_ref, wout_ref, wq_ref, xo_ref, h2_ref, pq_ref):
    o_dn = odf_ref[0] + odb_ref[0]
    parts = []
    for j in range(DN_W // LANES):
        oj = o_dn[:, j * LANES:(j + 1) * LANES]
        ms = _dot_exact_rhs(oj * oj, ones_ref[...]) * (1.0 / DN_DK)
        parts.append(oj * lax.rsqrt(ms + EPS))
    o_a = jnp.concatenate(parts, axis=1) * dnw_ref[...] * _silu(z_ref[0])
    y = (jax.nn.sigmoid(ga_ref[0]) * _dot(o_a.astype(BF16), wpa_ref[...])
         + jax.nn.sigmoid(gb_ref[0]) * _dot(ona_ref[0].astype(BF16), wpb_ref[...]))
    x = x_ref[0] + mod_ref[0, 0, 2:3, :] * _dot(y.astype(BF16), wout_ref[...])
    xo_ref[0] = x
    ms = jnp.mean(x * x, axis=-1, keepdims=True)
    h2 = (x * lax.rsqrt(ms + EPS) * n2w_ref[...]) * (1.0 + mod_ref[0, 0, 4:5, :]) + mod_ref[0, 0, 3:4, :]
    h2b = h2.astype(BF16)
    h2_ref[0] = h2b
    pq = _dot(h2b, wq_ref[...]).astype(BF16)
    for hp in range(pq_ref.shape[0]):
        pq_ref[hp, 0] = pq[:, hp * PEER_DK:(hp + 1) * PEER_DK]


def _merge(stream, o_dn_f, o_dn_b, z, o_na, ga, gb, modsel, dn_norm_w, norm2_w, w_pa, w_pb, w_out, w_q):
    b, s, d = stream.shape
    tm = TOK_TILE
    qw = w_q.shape[1]
    tok = lambda width: pl.BlockSpec((1, tm, width), lambda i, t: (i, t, 0))
    full = lambda arr: pl.BlockSpec(arr.shape, lambda i, t: (0,) * arr.ndim)
    dnw = jnp.tile(dn_norm_w, DN_HEADS).reshape(1, DN_W)
    n2w = norm2_w.reshape(1, d)
    ones = jnp.asarray(np.kron(np.eye(LANES // DN_DK), np.ones((DN_DK, DN_DK))), BF16)
    return pl.pallas_call(
        _merge_kernel,
        out_shape=[jax.ShapeDtypeStruct((b, s, d), F32), jax.ShapeDtypeStruct((b, s, d), BF16),
                   jax.ShapeDtypeStruct((qw // PEER_DK, b, s, PEER_DK), BF16)],
        grid=(b, s // tm),
        in_specs=[
            tok(d), tok(DN_W), tok(DN_W),
            tok(DN_W), tok(NA_W), tok(d), tok(d),
            pl.BlockSpec((1, 1, 8, d), lambda i, t: (i, jnp.minimum(t, 1), 0, 0)),
            full(dnw), full(n2w), full(ones), full(w_pa), full(w_pb), full(w_out), full(w_q),
        ],
        out_specs=[tok(d), tok(d),
                   pl.BlockSpec((qw // PEER_DK, 1, tm, PEER_DK), lambda i, t: (0, i, t, 0))],
        compiler_params=_cp(("arbitrary", "arbitrary")),
        name="merge_projection",
    )(stream, o_dn_f, o_dn_b, z, o_na, ga, gb, modsel, dnw, n2w, ones, w_pa, w_pb, w_out, w_q)


_LOG2E = 1.4426950408889634
_GELU_K1 = -2.0 * 0.7978845608028654 * _LOG2E
_GELU_K3 = _GELU_K1 * 0.044715


def _gelu_tanh(x):
    return x / (1.0 + jnp.exp2(x * (_GELU_K1 + _GELU_K3 * (x * x))))


def _peer_candidates(t1, t2, fill):
    sub = lax.broadcasted_iota(jnp.int32, (8, LANES), 0)
    cands = [t1[0:1] + t2[0:8], t1[0:1] + t2[8:16], t1[1:2] + t2[0:8]]
    for i in range(2, 8):
        cands.append(jnp.where(sub < PEER_TOPK // (i + 1), t1[i:i + 1] + t2[0:8], fill))
    cands.append(t1[8:16] + t2[0:1])
    return cands


def _peer_select_chunk(s1, s2, t1_ref, t2_ref):
    neg = jnp.float32(-jnp.inf)
    for t_ref, s in ((t1_ref, s1), (t2_ref, s2)):
        cur = s
        for kk in range(PEER_TOPK + 1):
            m = jnp.max(cur, axis=0, keepdims=True)
            t_ref[kk:kk + 1, :] = m
            if kk < PEER_TOPK:
                cur = jnp.where(cur == m, neg, cur)
    t1 = t1_ref[0:PEER_TOPK, :]
    t2 = t2_ref[0:PEER_TOPK, :]
    t1_next = t1_ref[PEER_TOPK:PEER_TOPK + 1, :]
    t2_next = t2_ref[PEER_TOPK:PEER_TOPK + 1, :]
    cands = _peer_candidates(t1, t2, neg)
    cur = cands
    tau = None
    for kk in range(PEER_TOPK):
        m = jnp.max(functools.reduce(jnp.maximum, cur), axis=0, keepdims=True)
        if kk + 1 < PEER_TOPK:
            cur = [jnp.where(cd == m, neg, cd) for cd in cur]
        else:
            tau = m
    m1 = t1[0:1]
    m2 = t2[0:1]
    top = m1 + m2
    z = functools.reduce(jnp.add, [
        jnp.sum(jnp.where(cd >= tau, jnp.exp(cd - top), 0.0), axis=0, keepdims=True) for cd in cands])
    log2_z = jnp.log(z) * _LOG2E
    to_p1 = lambda s: (s - m1) * _LOG2E - log2_z
    to_p2 = lambda s: (s - m2) * _LOG2E
    cands_p = _peer_candidates(to_p1(t1), to_p2(t2), neg)
    low = functools.reduce(jnp.minimum, [
        jnp.min(jnp.where(cd >= tau, cp, jnp.inf), axis=0, keepdims=True) for cd, cp in zip(cands, cands_p)])
    below = functools.reduce(jnp.maximum, [
        jnp.max(jnp.where(cd >= tau, neg, cp), axis=0, keepdims=True) for cd, cp in zip(cands, cands_p)])
    below = jnp.maximum(below, jnp.maximum(to_p1(t1[0:1]) + to_p2(t2_next), to_p1(t1_next) + to_p2(t2[0:1])))
    return to_p1(s1), to_p2(s2), 0.5 * (low + below)


def _peer_kernel(x_ref, h2_ref, pq_ref, g2_ref, keys_ref, u_ref, vt_ref, o_ref,
                 s_ref, th_ref, e1_ref, p2_ref, e2_ref, t1_ref, t2_ref, act_ref, g_ref, acc_ref, *, sub_tile):
    e = pl.program_id(1)
    tb = h2_ref.shape[0]
    n_chunk = tb // LANES
    eb = u_ref.shape[0]

    @pl.when(e == 0)
    def _():
        acc_ref[...] = jnp.zeros_like(acc_ref)
        for hp in range(2 * PEER_HEADS):
            st = _dot_nt(keys_ref[hp], pq_ref[hp])
            for ci in range(n_chunk):
                s_ref[hp, ci] = st[:, ci * LANES:(ci + 1) * LANES]

        def select_body(h, carry):
            for ci in range(n_chunk):
                p1, p2, thr = _peer_select_chunk(s_ref[2 * h, ci], s_ref[2 * h + 1, ci],
                                                 t1_ref.at[ci], t2_ref.at[ci])
                th_ref[h, ci] = thr - p1
                e1_ref[h, ci] = jnp.exp2(p1)
                p2_ref[h, ci] = p2
                e2_ref[h, ci] = jnp.exp2(p2)
            return carry

        lax.fori_loop(0, PEER_HEADS, select_body, 0)

    act_ref[...] = _gelu_tanh(_dot_nt(u_ref[...], h2_ref[...]))

    def gate_body(al, carry):
        a = e * (eb // N_KEYS) + al
        r0 = pl.multiple_of(al * N_KEYS, N_KEYS)
        for ci in range(n_chunk):
            cols = slice(ci * LANES, (ci + 1) * LANES)
            w = jnp.zeros((N_KEYS, LANES), F32)
            for h in range(PEER_HEADS):
                picked = p2_ref[h, ci] >= th_ref[h, ci, pl.ds(a, 1), :]
                w = w + jnp.where(picked, e2_ref[h, ci] * e1_ref[h, ci, pl.ds(a, 1), :], 0.0)
            g_ref[pl.ds(r0, N_KEYS), cols] = (w * act_ref[pl.ds(r0, N_KEYS), cols]).astype(BF16)
        return carry

    lax.fori_loop(0, eb // N_KEYS, gate_body, 0)
    acc_ref[...] += _dot(vt_ref[...], g_ref[...])

    @pl.when(e == pl.num_programs(1) - 1)
    def _():
        f = acc_ref[...].T
        for si in range(tb // sub_tile):
            rows = slice(si * sub_tile, (si + 1) * sub_tile)
            o_ref[rows, :] = x_ref[rows, :] + g2_ref[si] * f[rows, :]


def _peer(x_flat, h2_flat, pq_heads, g2_sub, keys_b, u_all, vt_all, layer):
    t, d = x_flat.shape
    tb = PEER_TB
    eb = PEER_EB
    n_exp = u_all.shape[1]
    n_blocks = n_exp // eb
    n_chunk = tb // LANES
    n_sub = tb // TOK_TILE
    tables = pltpu.VMEM((PEER_HEADS, n_chunk, N_KEYS, LANES), F32)
    return pl.pallas_call(
        functools.partial(_peer_kernel, sub_tile=TOK_TILE),
        out_shape=jax.ShapeDtypeStruct((t, d), F32),
        grid=(t // tb, n_blocks),
        in_specs=[
            pl.BlockSpec((tb, d), lambda i, e: (i, 0)),
            pl.BlockSpec((tb, d), lambda i, e: (i, 0)),
            pl.BlockSpec((2 * PEER_HEADS, tb, PEER_DK), lambda i, e: (0, i, 0)),
            pl.BlockSpec((n_sub, 1, d), lambda i, e: (i, 0, 0)),
            pl.BlockSpec(keys_b.shape, lambda i, e: (0, 0, 0)),
            pl.BlockSpec((None, eb, d), lambda i, e: (layer, e, 0)),
            pl.BlockSpec((None, d, eb), lambda i, e: (layer, 0, e)),
        ],
        out_specs=pl.BlockSpec((tb, d), lambda i, e: (i, 0)),
        scratch_shapes=[
            pltpu.VMEM((2 * PEER_HEADS, n_chunk, N_KEYS, LANES), F32),
            tables,
            tables,
            tables,
            tables,
            pltpu.VMEM((n_chunk, PEER_TOPK + 8, LANES), F32),
            pltpu.VMEM((n_chunk, PEER_TOPK + 8, LANES), F32),
            pltpu.VMEM((eb, tb), F32),
            pltpu.VMEM((eb, tb), BF16),
            pltpu.VMEM((d, tb), F32),
        ],
        compiler_params=_cp(("arbitrary", "arbitrary")),
        name="peer_dense",
    )(x_flat, h2_flat, pq_heads, g2_sub, keys_b, u_all, vt_all)


def _static_tables(n_ctx, n_lat):
    n_freq = DN_DK // 4
    t = np.arange(n_lat)
    row = (t // GRID_W).astype(np.float32)
    col = (t % GRID_W).astype(np.float32)
    inv_freq = (np.float32(ROPE_BASE) ** (-np.arange(n_freq, dtype=np.float32) / n_freq)).astype(np.float32)
    ang = np.concatenate([row[:, None] * inv_freq, col[:, None] * inv_freq], axis=-1)
    cos_h = np.concatenate([np.cos(ang), np.cos(ang)], axis=-1)
    sin_h = np.concatenate([-np.sin(ang), np.sin(ang)], axis=-1)
    cos_t = np.concatenate([np.ones((n_ctx, DN_DK), np.float32), cos_h], axis=0)
    sin_t = np.concatenate([np.zeros((n_ctx, DN_DK), np.float32), sin_h], axis=0)
    cos_t = jnp.asarray(np.tile(cos_t, (1, LANES // DN_DK)), F32)
    sin_t = jnp.asarray(np.tile(sin_t, (1, LANES // DN_DK)), F32)
    sel = np.zeros((2, 2, AB_W, DN_W), np.float32)
    for which in range(2):
        for dr in range(2):
            for h in range(DN_HEADS):
                sel[which, dr, which * 2 * DN_HEADS + dr * DN_HEADS + h, h * DN_DK:(h + 1) * DN_DK] = 1.0
    return cos_t, sin_t, jnp.asarray(np.concatenate([sel[0], sel[1]], axis=-1), BF16)


def _na_bias_table(rpb):
    c = np.arange(GRID_W)
    kc = np.arange(GRID_W)
    cs = np.clip(c - WIN_C // 2, 0, GRID_W - WIN_C)
    in_win = (kc[None, :] >= cs[:, None]) & (kc[None, :] < cs[:, None] + WIN_C)
    dc = np.clip(kc[None, :] - c[:, None] + (WIN_C - 1), 0, 2 * WIN_C - 2)
    var = np.arange(WIN_R)
    jj = np.arange(WIN_R)
    dr = var[:, None] + jj[None, :]
    tab = rpb[:, dr][:, :, :, dc]
    tab = jnp.where(in_win[None, None, None], tab, NEG_INF)
    tab = tab.transpose(0, 1, 3, 2, 4)
    return tab.reshape(rpb.shape[0], WIN_R, GRID_W, WIN_R * GRID_W)


def kernel(x, c, ctx, c_ctx, ada_w, ada_b, norm1_w, norm2_w, w_in, dn_conv_w, dn_a_log, dn_dt_bias,
           dn_norm_w, na_qnorm_w, na_knorm_w, na_rpb, w_pa, w_pb, w_out, peer_wq, peer_keys, peer_u,
           peer_v):
    b, n_lat, d = x.shape
    n_ctx = ctx.shape[1]
    depth = ada_w.shape[0]
    s = n_ctx + n_lat
    assert n_ctx % TOK_TILE == 0 and n_lat % TOK_TILE == 0 and (b * s) % PEER_TB == 0
    assert n_lat % GRID_W == 0 and n_lat // GRID_W >= WIN_R and d == 1024

    stream = jnp.concatenate([ctx, x], axis=1)

    cc = jnp.zeros((16, d), F32).at[:b].set(c).at[b].set(c_ctx)
    mod = _modulation(cc, ada_w, ada_b).reshape(depth, 16, N_MOD, d)
    mod = jnp.pad(mod, ((0, 0), (0, 0), (0, 8 - N_MOD), (0, 0)))
    modsel = jnp.stack([jnp.broadcast_to(mod[:, b][:, None], (depth, b, 8, d)), mod[:, :b]], axis=2)

    cos_t, sin_t, e_ab = _static_tables(n_ctx, n_lat)
    u_all = peer_u.astype(BF16)
    vt_all = jnp.swapaxes(peer_v, 1, 2).astype(BF16)

    for l in range(depth):
        wl = w_in[l]
        o1 = 3 * DN_W + DN_W
        o2 = o1 + 4 * DN_HEADS
        w_in_p = jnp.concatenate(
            [wl[:, :o1], wl[:, o1:o2], jnp.zeros((d, AB_W - 4 * DN_HEADS), F32), wl[:, o2:]],
            axis=1).astype(BF16)
        dqkv, z, ab, nqkv, ga, gb = _in_projection(stream, modsel[l], norm1_w[l], w_in_p)

        qkvn = _dn_prep(dqkv, dn_conv_w[l], cos_t, sin_t, n_ctx)
        neg_a = jnp.repeat(-jnp.exp(dn_a_log[l]), DN_DK, axis=-1).reshape(2, 1, DN_W)
        dt_b = jnp.repeat(dn_dt_bias[l], DN_DK, axis=-1).reshape(2, 1, DN_W)
        o_dn_f, o_dn_b = _dn_scan(qkvn, ab, e_ab, neg_a, dt_b, n_ctx)

        o_na = _na_attention(nqkv, na_qnorm_w[l], na_knorm_w[l], _na_bias_table(na_rpb[l]), n_ctx)

        stream, h2, pq = _merge(stream, o_dn_f, o_dn_b, z, o_na, ga, gb, modsel[l], dn_norm_w[l], norm2_w[l],
                                w_pa[l].astype(BF16), w_pb[l].astype(BF16), w_out[l].astype(BF16),
                                peer_wq[l].astype(BF16))

        g2_sub = jnp.broadcast_to(modsel[l][:, :, 5][:, jnp.minimum(jnp.arange(s // TOK_TILE), 1)],
                                  (b, s // TOK_TILE, d)).reshape(b * s // TOK_TILE, 1, d)
        keys_b = peer_keys[l].reshape(2 * PEER_HEADS, N_KEYS, PEER_DK).astype(BF16)
        out = _peer(stream.reshape(b * s, d), h2.reshape(b * s, d), pq.reshape(pq.shape[0], b * s, PEER_DK),
                    g2_sub, keys_b, u_all, vt_all, l)
        stream = out.reshape(b, s, d)

    return stream[:, n_ctx:]
```
